```python
import math
import jax
import jax.numpy as jnp
from jax import lax
import numpy as np

D_MODEL = 4096
BATCH = 4
SEQ = 2048
DEPTH = 4
DEC_BATCH = 8
DEC_SEQ = 8
PAST_LEN = 8192
PAGE_SIZE = 128

N_BRANCH = 4
BRANCH_W = D_MODEL // N_BRANCH
RW_HD = 64
RW_H = BRANCH_W // RW_HD
RW_W_RANK = 64
RW_A_RANK = 64
RW_G_RANK = 128
RW_COLS = 3 * BRANCH_W + RW_W_RANK + RW_A_RANK + RW_G_RANK
RW_GN_EPS = 64e-5
POOL_WINDOWS = (2, 4, 8, 16)
POOL_GW = BRANCH_W // len(POOL_WINDOWS)
POOL_BUF = max(POOL_WINDOWS) - 1
HG_DK = 128
HG_H = BRANCH_W // HG_DK
HG_DV = BRANCH_W // HG_H
HG_FDIM = HG_H * HG_DK
HG_COLS = 2 * HG_FDIM + 2 * BRANCH_W
HG_CHUNK = 64
HG_EPS = 1e-6
NSA_HD = 64
NSA_H = BRANCH_W // NSA_HD
NSA_G = 4
NSA_R = NSA_H // NSA_G
NSA_SCALE = NSA_HD ** -0.5
CMP_BLOCK = 32
CMP_STRIDE = 16
CMP_HIDDEN = 128
SEL_BLOCK = 64
N_SEL = 8
WINDOW = 512
Q_BLOCK = 128
NSA_KV_COLS = 6 * NSA_G * NSA_HD
NSA_COLS = BRANCH_W + NSA_KV_COLS + 3 * NSA_H
GATE_COLS = N_BRANCH * D_MODEL
IN_COLS = RW_COLS + BRANCH_W + HG_COLS + NSA_COLS + GATE_COLS
D_FF = 256 * ((8 * D_MODEL // 3 + 255) // 256)
CONV_W = 3
ALPHA = (2 * DEPTH) ** 0.25
BETA = (8 * DEPTH) ** -0.25
LN_EPS = 1e-5
NEG = -1e30

kernel_name = 'hybrid_rwkv_pool_hgrn_nsa_decode_step'


def _split(x, sizes):
    idx = tuple(int(s) for s in np.cumsum(sizes)[:-1])
    return jnp.split(x, idx, axis=-1)


def _layer_norm(x, w, b):
    xf = x.astype(jnp.float32)
    mu = xf.mean(-1, keepdims=True)
    var = jnp.mean(jnp.square(xf - mu), -1, keepdims=True)
    return ((xf - mu) * lax.rsqrt(var + LN_EPS) * w + b).astype(x.dtype)


def _rwkv7(p, shift_prev, S0, mu, w0, w2, a0, a2, g2, k_k, k_a, r_k, ln_w, ln_b):
    B, T, _ = p.shape
    f32 = jnp.float32
    p_prev = jnp.concatenate([shift_prev[:, None], p[:, :-1]], axis=1)
    xs = p + mu * (p_prev - p)
    r, k, v, wl, al, gl = _split(xs, (BRANCH_W, BRANCH_W, BRANCH_W, RW_W_RANK, RW_A_RANK, RW_G_RANK))
    w = -jax.nn.softplus(-(w0 + jnp.tanh(wl) @ w2).astype(f32)) - 0.5
    decay = jnp.exp(-jnp.exp(w))
    a = jax.nn.sigmoid((a0 + al @ a2).astype(f32))
    g = jax.nn.sigmoid(gl) @ g2

    def heads(t):
        return t.astype(f32).reshape(B, T, RW_H, RW_HD)

    kk = heads(k * k_k)
    kk = kk / jnp.maximum(jnp.linalg.norm(kk, axis=-1, keepdims=True), 1e-12)
    k2 = k.astype(f32) * (1.0 + (a - 1.0) * k_a)
    rh, kh, vh, ah, dh = heads(r), heads(k2), heads(v), heads(a), heads(decay)

    def step(S, inp):
        r_t, k_t, v_t, kk_t, a_t, d_t = inp
        sa = jnp.einsum('bhvk,bhk->bhv', S, -kk_t)
        S = (S * d_t[:, :, None, :] + sa[..., None] * (kk_t * a_t)[:, :, None, :]
             + v_t[..., None] * k_t[:, :, None, :])
        return S, jnp.einsum('bhvk,bhk->bhv', S, r_t)

    seq = tuple(jnp.moveaxis(t, 1, 0) for t in (rh, kh, vh, kk, ah, dh))
    S_T, y = lax.scan(step, S0.astype(f32), seq)
    y = jnp.moveaxis(y, 0, 1)
    mean = y.mean(-1, keepdims=True)
    var = jnp.mean(jnp.square(y - mean), -1, keepdims=True)
    yn = ((y - mean) * lax.rsqrt(var + RW_GN_EPS)).reshape(B, T, BRANCH_W) * ln_w + ln_b
    bonus = (jnp.sum(rh * kh * r_k, -1, keepdims=True) * vh).reshape(B, T, BRANCH_W)
    out = (yn + bonus) * g
    return out.astype(p.dtype), S_T.astype(S0.dtype), p[:, -1]


def _pool(p, buf, pos0, w_grp, scale):
    B, T, _ = p.shape
    xc = jnp.concatenate([buf, p], axis=1)
    cs = jnp.cumsum(xc.astype(jnp.float32), axis=1)
    cs = jnp.pad(cs, ((0, 0), (1, 0), (0, 0)))
    end = cs[:, POOL_BUF + 1: POOL_BUF + 1 + T]
    t = jnp.arange(T)
    outs = []
    for gi, w in enumerate(POOL_WINDOWS):
        lo, hi = gi * POOL_GW, (gi + 1) * POOL_GW
        start = cs[:, POOL_BUF + 1 - w: POOL_BUF + 1 - w + T, lo:hi]
        cnt = jnp.minimum(w, pos0 + t + 1).astype(jnp.float32)[None, :, None]
        outs.append((end[..., lo:hi] - start) / cnt - p[..., lo:hi].astype(jnp.float32))
    d = jnp.stack(outs, axis=2)
    y = jnp.einsum('btgc,gcd->btgd', d.astype(p.dtype), w_grp).reshape(B, T, BRANCH_W) * scale
    return y, xc[:, -POOL_BUF:]


def _hgrn2(p, S0, lb, norm_w):
    B, T, _ = p.shape
    f32 = jnp.float32
    q, f, i, g = _split(p, (HG_FDIM, HG_FDIM, BRANCH_W, BRANCH_W))
    ff = f.astype(f32).reshape(B, T, HG_H, HG_DK)
    lbh = lb.astype(f32).reshape(HG_H, HG_DK)
    log_f = jnp.logaddexp(jnp.log(lbh), jnp.log1p(-lbh) + jax.nn.log_sigmoid(ff))
    kk = (1.0 - lbh) * jax.nn.sigmoid(-ff)
    qq = jax.nn.silu(q.astype(f32)).reshape(B, T, HG_H, HG_DK)
    vv = i.astype(f32).reshape(B, T, HG_H, HG_DV)
    C = min(HG_CHUNK, T)
    nc = -(-T // C)
    pad = nc * C - T

    def chunks(t):
        t = jnp.pad(t, ((0, 0), (0, pad), (0, 0), (0, 0)))
        return t.reshape(B, nc, C, HG_H, t.shape[-1]).transpose(1, 0, 3, 2, 4)

    tri = jnp.tril(jnp.ones((C, C), dtype=bool))[:, :, None]

    def step(S, inp):
        qc, kc, vc, gc = inp
        b = jnp.cumsum(gc, axis=2)
        inter = jnp.einsum('bhck,bhkv->bhcv', qc * jnp.exp(b), S)
        diff = b[:, :, :, None, :] - b[:, :, None, :, :]
        dec = jnp.where(tri, jnp.exp(jnp.where(tri, diff, 0.0)), 0.0)
        att = jnp.einsum('bhtk,bhtsk,bhsk->bhts', qc, dec, kc)
        o = inter + jnp.einsum('bhts,bhsv->bhtv', att, vc)
        bl = b[:, :, -1]
        S = S * jnp.exp(bl)[..., None] + jnp.einsum('bhck,bhcv->bhkv', kc * jnp.exp(bl[:, :, None] - b), vc)
        return S, o

    S_T, o = lax.scan(step, S0.astype(f32), (chunks(qq), chunks(kk), chunks(vv), chunks(log_f)))
    o = o.transpose(1, 0, 3, 2, 4).reshape(B, nc * C, HG_H, HG_DV)[:, :T]
    o = o * lax.rsqrt(jnp.mean(jnp.square(o), -1, keepdims=True) + HG_EPS)
    o = o.reshape(B, T, BRANCH_W) * norm_w * jax.nn.silu(g.astype(f32))
    return o.astype(p.dtype), S_T.astype(S0.dtype)


def _compress(kv, pe, w1, w2):
    B, N = kv.shape[:2]
    nc = (N - CMP_BLOCK) // CMP_STRIDE + 1
    idx = jnp.arange(nc)[:, None] * CMP_STRIDE + jnp.arange(CMP_BLOCK)[None, :]
    blocks = kv[:, idx] + pe[None, None, :, None, :]
    blocks = blocks.transpose(0, 1, 3, 2, 4).reshape(B, nc, NSA_G, CMP_BLOCK * NSA_HD)
    return jax.nn.gelu(blocks @ w1) @ w2


def _overlap(nc, ns):
    cs = jnp.arange(nc)[:, None] * CMP_STRIDE
    ss = jnp.arange(ns)[None, :] * SEL_BLOCK
    return ((cs < ss + SEL_BLOCK) & (cs + CMP_BLOCK > ss)).astype(jnp.float32)


def _cmp_attention(qg, kc, vc, pos0):
    T = qg.shape[1]
    nc = kc.shape[1]
    s = jnp.einsum('btgrd,bcgd->bgrtc', qg, kc).astype(jnp.float32) * NSA_SCALE
    qpos = pos0 + jnp.arange(T)
    bend = jnp.arange(nc) * CMP_STRIDE + CMP_BLOCK - 1
    mask = bend[None, :] <= qpos[:, None]
    pr = jax.nn.softmax(jnp.where(mask, s, NEG), axis=-1) * mask
    o = jnp.einsum('bgrtc,bcgd->btgrd', pr.astype(vc.dtype), vc)
    return o, pr


def _sel_attention(qg, k, v, imp, pos0):
    B, T = qg.shape[:2]
    N = k.shape[1]
    ns = imp.shape[-1]
    n_sel = min(N_SEL, ns)
    padn = ns * SEL_BLOCK - N
    kb = jnp.pad(k, ((0, 0), (0, padn), (0, 0), (0, 0))).reshape(B, ns, SEL_BLOCK, NSA_G, NSA_HD).transpose(0, 3, 1, 2, 4)
    vb = jnp.pad(v, ((0, 0), (0, padn), (0, 0), (0, 0))).reshape(B, ns, SEL_BLOCK, NSA_G, NSA_HD).transpose(0, 3, 1, 2, 4)
    qpos = pos0 + jnp.arange(T)
    cur = qpos // SEL_BLOCK
    blk = jnp.arange(ns)
    sc = jnp.where(blk[None, :] <= cur[:, None], imp, -jnp.inf)
    sc = jnp.where(blk[None, :] == cur[:, None], jnp.inf, sc)
    _, idx = lax.top_k(sc, n_sel)
    Qb = min(Q_BLOCK, T)
    nqb = -(-T // Qb)
    Tp = nqb * Qb
    q_all = jnp.pad(qg, ((0, 0), (0, Tp - T), (0, 0), (0, 0), (0, 0))).reshape(B, nqb, Qb, NSA_G, NSA_R, NSA_HD).transpose(1, 0, 2, 3, 4, 5)
    idx_all = jnp.pad(idx, ((0, 0), (0, 0), (0, Tp - T), (0, 0))).reshape(B, NSA_G, nqb, Qb, n_sel).transpose(2, 0, 1, 3, 4)
    pos_all = (pos0 + jnp.arange(Tp)).reshape(nqb, Qb)
    bi = jnp.arange(B)[:, None, None, None]
    gi = jnp.arange(NSA_G)[None, :, None, None]
    lpos = jnp.arange(SEL_BLOCK)

    def block(inp):
        qb, ib, pb = inp
        kg = kb[bi, gi, ib]
        vg = vb[bi, gi, ib]
        s = jnp.einsum('bqgrd,bgqnld->bgrqnl', qb, kg).astype(jnp.float32) * NSA_SCALE
        kpos = ib[..., None] * SEL_BLOCK + lpos
        valid = kpos <= pb[None, None, :, None, None]
        s = jnp.where(valid[:, :, None], s, NEG)
        pr = jax.nn.softmax(s.reshape(B, NSA_G, NSA_R, Qb, n_sel * SEL_BLOCK), axis=-1).reshape(s.shape)
        return jnp.einsum('bgrqnl,bgqnld->bqgrd', pr.astype(vg.dtype), vg)

    o = lax.map(block, (q_all, idx_all, pos_all))
    return jnp.moveaxis(o, 0, 1).reshape(B, Tp, NSA_G, NSA_R, NSA_HD)[:, :T]


def _win_attention(qg, kv_new, kv_buf, pos0):
    B, T = qg.shape[:2]
    wb = kv_buf.shape[1]
    kv_cat = jnp.concatenate([kv_buf, kv_new], axis=1)
    Qb = min(Q_BLOCK, T)
    nqb = -(-T // Qb)
    Tp = nqb * Qb
    kv_pad = jnp.pad(kv_cat, ((0, 0), (0, Tp - T), (0, 0), (0, 0), (0, 0)))
    q_pad = jnp.pad(qg, ((0, 0), (0, Tp - T), (0, 0), (0, 0), (0, 0)))

    def block(c):
        start = c * Qb
        qs = lax.dynamic_slice_in_dim(q_pad, start, Qb, axis=1)
        kvs = lax.dynamic_slice_in_dim(kv_pad, start, wb + Qb, axis=1)
        qpos = pos0 + start + jnp.arange(Qb)
        kpos = pos0 - wb + start + jnp.arange(wb + Qb)
        rel = qpos[:, None] - kpos[None, :]
        mask = (rel >= 0) & (rel < WINDOW) & (kpos[None, :] >= 0)
        s = jnp.einsum('bqgrd,bkgd->bgrqk', qs, kvs[:, :, 0]).astype(jnp.float32) * NSA_SCALE
        pr = jax.nn.softmax(jnp.where(mask, s, NEG), axis=-1)
        return jnp.einsum('bgrqk,bkgd->bqgrd', pr.astype(kvs.dtype), kvs[:, :, 1])

    o = lax.map(block, jnp.arange(nqb))
    o = jnp.moveaxis(o, 0, 1).reshape(B, Tp, NSA_G, NSA_R, NSA_HD)[:, :T]
    return o, kv_cat


def _nsa(p, kv_past, win_buf, pos0, pe, w1, w2):
    B, T, _ = p.shape
    q, kv6, gl = _split(p, (BRANCH_W, NSA_KV_COLS, 3 * NSA_H))
    qg = q.reshape(B, T, NSA_G, NSA_R, NSA_HD)
    kv6 = kv6.reshape(B, T, 6, NSA_G, NSA_HD)
    kv_new = kv6[:, :, :4]
    kv_all = kv_new if kv_past is None else jnp.concatenate([kv_past, kv_new], axis=1)
    kc = _compress(kv_all[:, :, 0], pe[0], w1[0], w2[0])
    vc = _compress(kv_all[:, :, 1], pe[1], w1[1], w2[1])
    o_cmp, p_cmp = _cmp_attention(qg, kc, vc, pos0)
    ns = -(-kv_all.shape[1] // SEL_BLOCK)
    imp = jnp.einsum('bgrtc,cs->bgts', p_cmp, _overlap(kc.shape[1], ns))
    o_sel = _sel_attention(qg, kv_all[:, :, 2], kv_all[:, :, 3], imp, pos0)
    o_win, win_cat = _win_attention(qg, kv6[:, :, 4:], win_buf, pos0)
    g = jax.nn.sigmoid(gl.reshape(B, T, 3, NSA_G, NSA_R))[..., None]
    o = g[:, :, 0] * o_cmp + g[:, :, 1] * o_sel + g[:, :, 2] * o_win
    return o.reshape(B, T, BRANCH_W), kv_new, win_cat


def _conv_ffn(x, buf, w_up, w_conv, w_down):
    T = x.shape[1]
    u, v = jnp.split(x @ w_up, 2, axis=-1)
    uc = jnp.concatenate([buf, u], axis=1)
    c = w_conv[0] * uc[:, 0:T]
    for j in range(1, CONV_W):
        c = c + w_conv[j] * uc[:, j:j + T]
    h = jax.nn.gelu(c) * v
    return h @ w_down, uc[:, -(CONV_W - 1):]


def setup_inputs(seed: int = 0) -> dict:
    key = jax.random.key(seed)
    ks = iter(jax.random.split(key, 48))

    def nrm(shape, s):
        return s * jax.random.normal(next(ks), shape, jnp.float32)

    def uni(shape, lo, hi):
        return jax.random.uniform(next(ks), shape, jnp.float32, lo, hi)

    n_pages = PAST_LEN // PAGE_SIZE
    n_used = DEC_BATCH * n_pages
    n_phys = n_used + max(1, n_used // 4)
    win = min(WINDOW, PAST_LEN)
    perm = jax.random.permutation(next(ks), n_phys)
    page_table = perm[:n_used].reshape(DEC_BATCH, n_pages).astype(jnp.int32)
    return {
        'x_prompt': nrm((BATCH, SEQ, D_MODEL), 1.0),
        'x_sample': nrm((DEC_BATCH, DEC_SEQ, D_MODEL), 1.0),
        'cache_kv': nrm((DEPTH, n_phys, PAGE_SIZE, 4, NSA_G, NSA_HD), 1.0),
        'page_table': page_table,
        'state_win': nrm((DEPTH, DEC_BATCH, win, 2, NSA_G, NSA_HD), 1.0),
        'state_rwkv': nrm((DEPTH, DEC_BATCH, RW_H, RW_HD, RW_HD), 0.3),
        'state_rwkv_shift': nrm((DEPTH, DEC_BATCH, RW_COLS), 1.0),
        'state_pool': nrm((DEPTH, DEC_BATCH, POOL_BUF, BRANCH_W), 1.0),
        'state_hgrn': nrm((DEPTH, DEC_BATCH, HG_H, HG_DK, HG_DV), 0.3),
        'state_ffn_conv': nrm((DEPTH, DEC_BATCH, CONV_W - 1, D_FF), 1.0),
        'w_in': nrm((DEPTH, D_MODEL, IN_COLS), D_MODEL ** -0.5),
        'rw_mu': uni((DEPTH, RW_COLS), 0.0, 1.0),
        'rw_w0': uni((DEPTH, BRANCH_W), -6.0, 0.0),
        'rw_w2': nrm((DEPTH, RW_W_RANK, BRANCH_W), 0.1),
        'rw_a0': nrm((DEPTH, BRANCH_W), 0.5),
        'rw_a2': nrm((DEPTH, RW_A_RANK, BRANCH_W), 0.1),
        'rw_g2': nrm((DEPTH, RW_G_RANK, BRANCH_W), RW_G_RANK ** -0.5),
        'rw_k_k': 0.85 + nrm((DEPTH, BRANCH_W), 0.05),
        'rw_k_a': 1.0 + nrm((DEPTH, BRANCH_W), 0.05),
        'rw_r_k': nrm((DEPTH, RW_H, RW_HD), 0.1),
        'rw_ln_w': 1.0 + nrm((DEPTH, BRANCH_W), 0.05),
        'rw_ln_b': nrm((DEPTH, BRANCH_W), 0.02),
        'pool_w': nrm((DEPTH, len(POOL_WINDOWS), POOL_GW, POOL_GW), POOL_GW ** -0.5),
        'pool_scale': 1.0 + nrm((DEPTH, BRANCH_W), 0.1),
        'hg_lb_raw': nrm((DEPTH, HG_FDIM), 1.0),
        'hg_norm_w': 1.0 + nrm((DEPTH, BRANCH_W), 0.05),
        'cmp_pe': nrm((DEPTH, 2, CMP_BLOCK, NSA_HD), 0.1),
        'cmp_w1': nrm((DEPTH, 2, CMP_BLOCK * NSA_HD, CMP_HIDDEN), (CMP_BLOCK * NSA_HD) ** -0.5),
        'cmp_w2': nrm((DEPTH, 2, CMP_HIDDEN, NSA_HD), CMP_HIDDEN ** -0.5),
        'w_branch': nrm((DEPTH, N_BRANCH, BRANCH_W, D_MODEL), BRANCH_W ** -0.5),
        'w_out': nrm((DEPTH, D_MODEL, D_MODEL), BETA * D_MODEL ** -0.5),
        'ln_w': 1.0 + nrm((DEPTH, 2, D_MODEL), 0.05),
        'ln_b': nrm((DEPTH, 2, D_MODEL), 0.02),
        'ffn_up': nrm((DEPTH, D_MODEL, 2 * D_FF), D_MODEL ** -0.5),
        'ffn_conv': nrm((DEPTH, CONV_W, D_FF), CONV_W ** -0.5),
        'ffn_down': nrm((DEPTH, D_FF, D_MODEL), BETA * D_FF ** -0.5),
    }


def reference(x_prompt, x_sample, cache_kv, page_table, state_win, state_rwkv, state_rwkv_shift,
              state_pool, state_hgrn, state_ffn_conv, w_in, rw_mu, rw_w0, rw_w2, rw_a0, rw_a2,
              rw_g2, rw_k_k, rw_k_a, rw_r_k, rw_ln_w, rw_ln_b, pool_w, pool_scale, hg_lb_raw,
              hg_norm_w, cmp_pe, cmp_w1, cmp_w2, w_branch, w_out, ln_w, ln_b, ffn_up, ffn_conv,
              ffn_down):
    lb_cum = jnp.cumsum(jax.nn.softmax(hg_lb_raw.astype(jnp.float32), axis=0), axis=0)
    lb_all = lb_cum - lb_cum[:1]
    past_len = page_table.shape[1] * cache_kv.shape[2]
    win_keep = state_win.shape[2]

    def layer(l, x, pos0, kv_past, win_buf, rw_S, rw_shift, pool_buf, hg_S, ffn_buf):
        B, T, _ = x.shape
        proj = x @ w_in[l]
        p_rw, p_pool, p_hg, p_nsa, p_gate = _split(proj, (RW_COLS, BRANCH_W, HG_COLS, NSA_COLS, GATE_COLS))
        h_rw, rw_S_new, rw_shift_new = _rwkv7(p_rw, rw_shift, rw_S, rw_mu[l], rw_w0[l], rw_w2[l],
                                              rw_a0[l], rw_a2[l], rw_g2[l], rw_k_k[l], rw_k_a[l],
                                              rw_r_k[l], rw_ln_w[l], rw_ln_b[l])
        h_pool, pool_new = _pool(p_pool, pool_buf, pos0, pool_w[l], pool_scale[l])
        h_hg, hg_S_new = _hgrn2(p_hg, hg_S, lb_all[l], hg_norm_w[l])
        h_nsa, kv_new, win_cat = _nsa(p_nsa, kv_past, win_buf, pos0, cmp_pe[l], cmp_w1[l], cmp_w2[l])
        win_new = win_cat[:, -win_keep:]
        gates = jax.nn.sigmoid(p_gate.reshape(B, T, N_BRANCH, D_MODEL))
        branches = (h_rw, h_pool, h_hg, h_nsa)
        merged = gates[:, :, 0] * (branches[0] @ w_branch[l, 0])
        for n in range(1, N_BRANCH):
            merged = merged + gates[:, :, n] * (branches[n] @ w_branch[l, n])
        y = merged @ w_out[l]
        x = _layer_norm(ALPHA * x + y, ln_w[l, 0], ln_b[l, 0])
        f, ffn_new = _conv_ffn(x, ffn_buf, ffn_up[l], ffn_conv[l], ffn_down[l])
        x = _layer_norm(ALPHA * x + f, ln_w[l, 1], ln_b[l, 1])
        return x, (kv_new, win_new, rw_S_new, rw_shift_new, pool_new, hg_S_new, ffn_new)

    bp = x_prompt.shape[0]
    dt = x_prompt.dtype
    z_win = jnp.zeros((bp, WINDOW, 2, NSA_G, NSA_HD), dt)
    z_rw = jnp.zeros((bp, RW_H, RW_HD, RW_HD), dt)
    z_shift = jnp.zeros((bp, RW_COLS), dt)
    z_pool = jnp.zeros((bp, POOL_BUF, BRANCH_W), dt)
    z_hg = jnp.zeros((bp, HG_H, HG_DK, HG_DV), dt)
    z_ffn = jnp.zeros((bp, CONV_W - 1, D_FF), dt)
    xp = x_prompt
    new_p = []
    for l in range(DEPTH):
        xp, st = layer(l, xp, 0, None, z_win, z_rw, z_shift, z_pool, z_hg, z_ffn)
        new_p.append(st)

    db = x_sample.shape[0]
    xs = x_sample
    new_s = []
    for l in range(DEPTH):
        kv_past = cache_kv[l][page_table].reshape(db, past_len, 4, NSA_G, NSA_HD)
        xs, st = layer(l, xs, past_len, kv_past, state_win[l], state_rwkv[l], state_rwkv_shift[l],
                       state_pool[l], state_hgrn[l], state_ffn_conv[l])
        new_s.append(st)

    sp = [jnp.stack([st[i] for st in new_p]) for i in range(7)]
    ss = [jnp.stack([st[i] for st in new_s]) for i in range(7)]
    return (xp, xs, sp[0], ss[0], sp[1], ss[1], sp[2], ss[2], sp[3], ss[3], sp[4], ss[4], sp[5], ss[5], sp[6], ss[6])
```

```python
import functools
import math

import jax
import jax.numpy as jnp
import numpy as np
from jax import lax
from jax.experimental import pallas as pl
from jax.experimental.pallas import tpu as pltpu

D_MODEL = 4096
DEPTH = 4
N_BRANCH = 4
BRANCH_W = D_MODEL // N_BRANCH
RW_HD = 64
RW_H = BRANCH_W // RW_HD
RW_W_RANK = 64
RW_A_RANK = 64
RW_G_RANK = 128
RW_COLS = 3 * BRANCH_W + RW_W_RANK + RW_A_RANK + RW_G_RANK
RW_GN_EPS = 64e-5
POOL_WINDOWS = (2, 4, 8, 16)
POOL_GW = BRANCH_W // len(POOL_WINDOWS)
POOL_BUF = max(POOL_WINDOWS) - 1
HG_DK = 128
HG_H = BRANCH_W // HG_DK
HG_DV = BRANCH_W // HG_H
HG_FDIM = HG_H * HG_DK
HG_COLS = 2 * HG_FDIM + 2 * BRANCH_W
HG_CHUNK = 64
HG_EPS = 1e-6
NSA_HD = 64
NSA_H = BRANCH_W // NSA_HD
NSA_G = 4
NSA_R = NSA_H // NSA_G
NSA_SCALE = NSA_HD ** -0.5
CMP_BLOCK = 32
CMP_STRIDE = 16
CMP_HIDDEN = 128
SEL_BLOCK = 64
N_SEL = 8
WINDOW = 512
Q_BLOCK = 128
NSA_KV_COLS = 6 * NSA_G * NSA_HD
NSA_COLS = BRANCH_W + NSA_KV_COLS + 3 * NSA_H
GATE_COLS = N_BRANCH * D_MODEL
IN_COLS = RW_COLS + BRANCH_W + HG_COLS + NSA_COLS + GATE_COLS
D_FF = 256 * ((8 * D_MODEL // 3 + 255) // 256)
CONV_W = 3
ALPHA = (2 * DEPTH) ** 0.25
LN_EPS = 1e-5
NEG = -1e30

MIX_COLS = RW_COLS + BRANCH_W + HG_COLS + NSA_COLS
IN_TN = 1024
MIX_PAD = -(-MIX_COLS // IN_TN) * IN_TN
IN_PACKED = MIX_PAD + GATE_COLS

VMEM_LIMIT = 56 * 1024 * 1024


def _mm_kernel(a_ref, b_ref, o_ref):
    o_ref[...] = jnp.dot(a_ref[...], b_ref[...], preferred_element_type=jnp.float32).astype(o_ref.dtype)


def _matmul(a, b, tm, tn, out_dtype=jnp.float32):
    m, k = a.shape
    _, n = b.shape
    tm = min(tm, m)
    assert m % tm == 0 and n % tn == 0, (m, n, tm, tn)
    return pl.pallas_call(
        _mm_kernel,
        grid=(m // tm, n // tn),
        in_specs=[pl.BlockSpec((tm, k), lambda i, j: (i, 0)),
                  pl.BlockSpec((k, tn), lambda i, j: (0, j))],
        out_specs=pl.BlockSpec((tm, tn), lambda i, j: (i, j)),
        out_shape=jax.ShapeDtypeStruct((m, n), out_dtype),
        compiler_params=pltpu.CompilerParams(
            dimension_semantics=("arbitrary", "arbitrary"), vmem_limit_bytes=VMEM_LIMIT),
        name="matmul",
    )(a, b)


def _mm_ksplit_kernel(a_ref, b_ref, o_ref):
    @pl.when(pl.program_id(2) == 0)
    def _():
        o_ref[...] = jnp.zeros_like(o_ref)

    o_ref[...] += jnp.dot(a_ref[...], b_ref[...], preferred_element_type=jnp.float32)


def _matmul_ksplit(a, b, tm, tn, tk):
    m, k = a.shape
    _, n = b.shape
    tm = min(tm, m)
    assert m % tm == 0 and n % tn == 0 and k % tk == 0
    return pl.pallas_call(
        _mm_ksplit_kernel,
        grid=(m // tm, n // tn, k // tk),
        in_specs=[pl.BlockSpec((tm, tk), lambda i, j, kk: (i, kk)),
                  pl.BlockSpec((tk, tn), lambda i, j, kk: (kk, j))],
        out_specs=pl.BlockSpec((tm, tn), lambda i, j, kk: (i, j)),
        out_shape=jax.ShapeDtypeStruct((m, n), jnp.float32),
        compiler_params=pltpu.CompilerParams(
            dimension_semantics=("arbitrary", "arbitrary", "arbitrary"), vmem_limit_bytes=VMEM_LIMIT),
        name="matmul_ksplit",
    )(a, b)


def _merge_kernel(h_ref, w_ref, g_ref, o_ref):
    n = pl.program_id(2)

    @pl.when(n == 0)
    def _():
        o_ref[...] = jnp.zeros_like(o_ref)

    y = jnp.dot(h_ref[0], w_ref[0], preferred_element_type=jnp.float32)
    o_ref[...] += jax.nn.sigmoid(g_ref[...]) * y


def _gated_merge(h, w, proj, tm, tn):
    nb, m, bw = h.shape
    d = w.shape[-1]
    tm = min(tm, m)
    goff = MIX_PAD // tn
    nj = d // tn
    return pl.pallas_call(
        _merge_kernel,
        grid=(m // tm, nj, nb),
        in_specs=[pl.BlockSpec((1, tm, bw), lambda i, j, n: (n, i, 0)),
                  pl.BlockSpec((1, bw, tn), lambda i, j, n: (n, 0, j)),
                  pl.BlockSpec((tm, tn), lambda i, j, n: (i, goff + n * nj + j))],
        out_specs=pl.BlockSpec((tm, tn), lambda i, j, n: (i, j)),
        out_shape=jax.ShapeDtypeStruct((m, d), jnp.float32),
        compiler_params=pltpu.CompilerParams(
            dimension_semantics=("arbitrary", "arbitrary", "arbitrary"), vmem_limit_bytes=VMEM_LIMIT),
        name="gated_merge",
    )(h, w, proj)


def _split(x, sizes):
    idx = tuple(int(s) for s in np.cumsum(sizes)[:-1])
    return jnp.split(x, idx, axis=-1)


def _layer_norm(x, w, b):
    mu = x.mean(-1, keepdims=True)
    var = jnp.mean(jnp.square(x - mu), -1, keepdims=True)
    return (x - mu) * lax.rsqrt(var + LN_EPS) * w + b


def _rwkv7(p, shift_prev, S0, mu, w0, w2, a0, a2, g2, k_k, k_a, r_k, ln_w, ln_b):
    B, T, _ = p.shape
    f32 = jnp.float32
    p_prev = jnp.concatenate([shift_prev[:, None], p[:, :-1]], axis=1)
    xs = p + mu * (p_prev - p)
    r, k, v, wl, al, gl = _split(xs, (BRANCH_W, BRANCH_W, BRANCH_W, RW_W_RANK, RW_A_RANK, RW_G_RANK))
    w = -jax.nn.softplus(-(w0 + jnp.tanh(wl) @ w2).astype(f32)) - 0.5
    decay = jnp.exp(-jnp.exp(w))
    a = jax.nn.sigmoid((a0 + al @ a2).astype(f32))
    g = jax.nn.sigmoid(gl) @ g2

    def heads(t):
        return t.astype(f32).reshape(B, T, RW_H, RW_HD)

    kk = heads(k * k_k)
    kk = kk / jnp.maximum(jnp.linalg.norm(kk, axis=-1, keepdims=True), 1e-12)
    k2 = k.astype(f32) * (1.0 + (a - 1.0) * k_a)
    rh, kh, vh, ah, dh = heads(r), heads(k2), heads(v), heads(a), heads(decay)

    def step(S, inp):
        r_t, k_t, v_t, kk_t, a_t, d_t = inp
        sa = jnp.einsum('bhvk,bhk->bhv', S, -kk_t)
        S = (S * d_t[:, :, None, :] + sa[..., None] * (kk_t * a_t)[:, :, None, :]
             + v_t[..., None] * k_t[:, :, None, :])
        return S, jnp.einsum('bhvk,bhk->bhv', S, r_t)

    seq = tuple(jnp.moveaxis(t, 1, 0) for t in (rh, kh, vh, kk, ah, dh))
    S_T, y = lax.scan(step, S0.astype(f32), seq)
    y = jnp.moveaxis(y, 0, 1)
    mean = y.mean(-1, keepdims=True)
    var = jnp.mean(jnp.square(y - mean), -1, keepdims=True)
    yn = ((y - mean) * lax.rsqrt(var + RW_GN_EPS)).reshape(B, T, BRANCH_W) * ln_w + ln_b
    bonus = (jnp.sum(rh * kh * r_k, -1, keepdims=True) * vh).reshape(B, T, BRANCH_W)
    out = (yn + bonus) * g
    return out, S_T, p[:, -1]


def _pool(p, buf, pos0, w_grp, scale):
    B, T, _ = p.shape
    xc = jnp.concatenate([buf, p], axis=1)
    cs = jnp.cumsum(xc.astype(jnp.float32), axis=1)
    cs = jnp.pad(cs, ((0, 0), (1, 0), (0, 0)))
    end = cs[:, POOL_BUF + 1: POOL_BUF + 1 + T]
    t = jnp.arange(T)
    outs = []
    for gi, w in enumerate(POOL_WINDOWS):
        lo, hi = gi * POOL_GW, (gi + 1) * POOL_GW
        start = cs[:, POOL_BUF + 1 - w: POOL_BUF + 1 - w + T, lo:hi]
        cnt = jnp.minimum(w, pos0 + t + 1).astype(jnp.float32)[None, :, None]
        outs.append((end[..., lo:hi] - start) / cnt - p[..., lo:hi].astype(jnp.float32))
    d = jnp.stack(outs, axis=2)
    y = jnp.einsum('btgc,gcd->btgd', d.astype(p.dtype), w_grp).reshape(B, T, BRANCH_W) * scale
    return y, xc[:, -POOL_BUF:]


def _hgrn2(p, S0, lb, norm_w):
    B, T, _ = p.shape
    f32 = jnp.float32
    q, f, i, g = _split(p, (HG_FDIM, HG_FDIM, BRANCH_W, BRANCH_W))
    ff = f.astype(f32).reshape(B, T, HG_H, HG_DK)
    lbh = lb.astype(f32).reshape(HG_H, HG_DK)
    log_f = jnp.logaddexp(jnp.log(lbh), jnp.log1p(-lbh) + jax.nn.log_sigmoid(ff))
    kk = (1.0 - lbh) * jax.nn.sigmoid(-ff)
    qq = jax.nn.silu(q.astype(f32)).reshape(B, T, HG_H, HG_DK)
    vv = i.astype(f32).reshape(B, T, HG_H, HG_DV)
    C = min(HG_CHUNK, T)
    nc = -(-T // C)
    pad = nc * C - T

    def chunks(t):
        t = jnp.pad(t, ((0, 0), (0, pad), (0, 0), (0, 0)))
        return t.reshape(B, nc, C, HG_H, t.shape[-1]).transpose(1, 0, 3, 2, 4)

    tri = jnp.tril(jnp.ones((C, C), dtype=bool))[:, :, None]

    def step(S, inp):
        qc, kc, vc, gc = inp
        b = jnp.cumsum(gc, axis=2)
        inter = jnp.einsum('bhck,bhkv->bhcv', qc * jnp.exp(b), S)
        diff = b[:, :, :, None, :] - b[:, :, None, :, :]
        dec = jnp.where(tri, jnp.exp(jnp.where(tri, diff, 0.0)), 0.0)
        att = jnp.einsum('bhtk,bhtsk,bhsk->bhts', qc, dec, kc)
        o = inter + jnp.einsum('bhts,bhsv->bhtv', att, vc)
        bl = b[:, :, -1]
        S = S * jnp.exp(bl)[..., None] + jnp.einsum('bhck,bhcv->bhkv', kc * jnp.exp(bl[:, :, None] - b), vc)
        return S, o

    S_T, o = lax.scan(step, S0.astype(f32), (chunks(qq), chunks(kk), chunks(vv), chunks(log_f)))
    o = o.transpose(1, 0, 3, 2, 4).reshape(B, nc * C, HG_H, HG_DV)[:, :T]
    o = o * lax.rsqrt(jnp.mean(jnp.square(o), -1, keepdims=True) + HG_EPS)
    o = o.reshape(B, T, BRANCH_W) * norm_w * jax.nn.silu(g.astype(f32))
    return o, S_T


def _compress(kv, pe, w1, w2):
    B, N = kv.shape[:2]
    nc = (N - CMP_BLOCK) // CMP_STRIDE + 1
    idx = jnp.arange(nc)[:, None] * CMP_STRIDE + jnp.arange(CMP_BLOCK)[None, :]
    blocks = kv[:, idx] + pe[None, None, :, None, :]
    blocks = blocks.transpose(0, 1, 3, 2, 4).reshape(B, nc, NSA_G, CMP_BLOCK * NSA_HD)
    return jax.nn.gelu(blocks @ w1) @ w2


def _overlap(nc, ns):
    cs = jnp.arange(nc)[:, None] * CMP_STRIDE
    ss = jnp.arange(ns)[None, :] * SEL_BLOCK
    return ((cs < ss + SEL_BLOCK) & (cs + CMP_BLOCK > ss)).astype(jnp.float32)


def _cmp_attention(qg, kc, vc, pos0):
    T = qg.shape[1]
    nc = kc.shape[1]
    s = jnp.einsum('btgrd,bcgd->bgrtc', qg, kc).astype(jnp.float32) * NSA_SCALE
    qpos = pos0 + jnp.arange(T)
    bend = jnp.arange(nc) * CMP_STRIDE + CMP_BLOCK - 1
    mask = bend[None, :] <= qpos[:, None]
    pr = jax.nn.softmax(jnp.where(mask, s, NEG), axis=-1) * mask
    o = jnp.einsum('bgrtc,bcgd->btgrd', pr.astype(vc.dtype), vc)
    return o, pr


def _sel_attention(qg, k, v, imp, pos0):
    B, T = qg.shape[:2]
    N = k.shape[1]
    ns = imp.shape[-1]
    n_sel = min(N_SEL, ns)
    padn = ns * SEL_BLOCK - N
    kb = jnp.pad(k, ((0, 0), (0, padn), (0, 0), (0, 0))).reshape(B, ns, SEL_BLOCK, NSA_G, NSA_HD).transpose(0, 3, 1, 2, 4)
    vb = jnp.pad(v, ((0, 0), (0, padn), (0, 0), (0, 0))).reshape(B, ns, SEL_BLOCK, NSA_G, NSA_HD).transpose(0, 3, 1, 2, 4)
    qpos = pos0 + jnp.arange(T)
    cur = qpos // SEL_BLOCK
    blk = jnp.arange(ns)
    sc = jnp.where(blk[None, :] <= cur[:, None], imp, -jnp.inf)
    sc = jnp.where(blk[None, :] == cur[:, None], jnp.inf, sc)
    _, idx = lax.top_k(sc, n_sel)
    Qb = min(Q_BLOCK, T)
    nqb = -(-T // Qb)
    Tp = nqb * Qb
    q_all = jnp.pad(qg, ((0, 0), (0, Tp - T), (0, 0), (0, 0), (0, 0))).reshape(B, nqb, Qb, NSA_G, NSA_R, NSA_HD).transpose(1, 0, 2, 3, 4, 5)
    idx_all = jnp.pad(idx, ((0, 0), (0, 0), (0, Tp - T), (0, 0))).reshape(B, NSA_G, nqb, Qb, n_sel).transpose(2, 0, 1, 3, 4)
    pos_all = (pos0 + jnp.arange(Tp)).reshape(nqb, Qb)
    bi = jnp.arange(B)[:, None, None, None]
    gi = jnp.arange(NSA_G)[None, :, None, None]
    lpos = jnp.arange(SEL_BLOCK)

    def block(inp):
        qb, ib, pb = inp
        kg = kb[bi, gi, ib]
        vg = vb[bi, gi, ib]
        s = jnp.einsum('bqgrd,bgqnld->bgrqnl', qb, kg).astype(jnp.float32) * NSA_SCALE
        kpos = ib[..., None] * SEL_BLOCK + lpos
        valid = kpos <= pb[None, None, :, None, None]
        s = jnp.where(valid[:, :, None], s, NEG)
        pr = jax.nn.softmax(s.reshape(B, NSA_G, NSA_R, Qb, n_sel * SEL_BLOCK), axis=-1).reshape(s.shape)
        return jnp.einsum('bgrqnl,bgqnld->bqgrd', pr.astype(vg.dtype), vg)

    o = lax.map(block, (q_all, idx_all, pos_all))
    return jnp.moveaxis(o, 0, 1).reshape(B, Tp, NSA_G, NSA_R, NSA_HD)[:, :T]


def _win_attention(qg, kv_new, kv_buf, pos0):
    B, T = qg.shape[:2]
    wb = kv_buf.shape[1]
    kv_cat = jnp.concatenate([kv_buf, kv_new], axis=1)
    Qb = min(Q_BLOCK, T)
    nqb = -(-T // Qb)
    Tp = nqb * Qb
    kv_pad = jnp.pad(kv_cat, ((0, 0), (0, Tp - T), (0, 0), (0, 0), (0, 0)))
    q_pad = jnp.pad(qg, ((0, 0), (0, Tp - T), (0, 0), (0, 0), (0, 0)))

    def block(c):
        start = c * Qb
        qs = lax.dynamic_slice_in_dim(q_pad, start, Qb, axis=1)
        kvs = lax.dynamic_slice_in_dim(kv_pad, start, wb + Qb, axis=1)
        qpos = pos0 + start + jnp.arange(Qb)
        kpos = pos0 - wb + start + jnp.arange(wb + Qb)
        rel = qpos[:, None] - kpos[None, :]
        mask = (rel >= 0) & (rel < WINDOW) & (kpos[None, :] >= 0)
        s = jnp.einsum('bqgrd,bkgd->bgrqk', qs, kvs[:, :, 0]).astype(jnp.float32) * NSA_SCALE
        pr = jax.nn.softmax(jnp.where(mask, s, NEG), axis=-1)
        return jnp.einsum('bgrqk,bkgd->bqgrd', pr.astype(kvs.dtype), kvs[:, :, 1])

    o = lax.map(block, jnp.arange(nqb))
    o = jnp.moveaxis(o, 0, 1).reshape(B, Tp, NSA_G, NSA_R, NSA_HD)[:, :T]
    return o, kv_cat


def _nsa(p, kv_past, win_buf, pos0, pe, w1, w2):
    B, T, _ = p.shape
    q, kv6, gl = _split(p, (BRANCH_W, NSA_KV_COLS, 3 * NSA_H))
    qg = q.reshape(B, T, NSA_G, NSA_R, NSA_HD)
    kv6 = kv6.reshape(B, T, 6, NSA_G, NSA_HD)
    kv_new = kv6[:, :, :4]
    kv_all = kv_new if kv_past is None else jnp.concatenate([kv_past, kv_new], axis=1)
    kc = _compress(kv_all[:, :, 0], pe[0], w1[0], w2[0])
    vc = _compress(kv_all[:, :, 1], pe[1], w1[1], w2[1])
    o_cmp, p_cmp = _cmp_attention(qg, kc, vc, pos0)
    ns = -(-kv_all.shape[1] // SEL_BLOCK)
    imp = jnp.einsum('bgrtc,cs->bgts', p_cmp, _overlap(kc.shape[1], ns))
    o_sel = _sel_attention(qg, kv_all[:, :, 2], kv_all[:, :, 3], imp, pos0)
    o_win, win_cat = _win_attention(qg, kv6[:, :, 4:], win_buf, pos0)
    g = jax.nn.sigmoid(gl.reshape(B, T, 3, NSA_G, NSA_R))[..., None]
    o = g[:, :, 0] * o_cmp + g[:, :, 1] * o_sel + g[:, :, 2] * o_win
    return o.reshape(B, T, BRANCH_W), kv_new, win_cat


def _pack_weights(w_in, w_branch, w_out, ffn_up, ffn_down):
    bf = jnp.bfloat16
    w_in_p = jnp.concatenate(
        [w_in[:, :, :MIX_COLS].astype(bf),
         jnp.zeros((DEPTH, D_MODEL, MIX_PAD - MIX_COLS), bf),
         w_in[:, :, MIX_COLS:].astype(bf)], axis=-1)
    return w_in_p, w_branch.astype(bf), w_out.astype(bf), ffn_up.astype(bf), ffn_down.astype(bf)


def _layer(l, x, pos0, kv_past, win_buf, rw_S, rw_shift, pool_buf, hg_S, ffn_buf, wts, prm, lb_all, win_keep):
    (w_in_p, w_branch_b, w_out_b, ffn_up_b, ffn_down_b) = wts
    B, T, _ = x.shape
    m = B * T
    bf = jnp.bfloat16
    x2 = x.reshape(m, D_MODEL)
    proj = _matmul(x2.astype(bf), w_in_p[l], 1024, IN_TN)
    pm = proj[:, :MIX_COLS].reshape(B, T, MIX_COLS)
    p_rw, p_pool, p_hg, p_nsa = _split(pm, (RW_COLS, BRANCH_W, HG_COLS, NSA_COLS))
    h_rw, rw_S_new, rw_shift_new = _rwkv7(p_rw, rw_shift, rw_S, prm['rw_mu'][l], prm['rw_w0'][l], prm['rw_w2'][l],
                                          prm['rw_a0'][l], prm['rw_a2'][l], prm['rw_g2'][l], prm['rw_k_k'][l],
                                          prm['rw_k_a'][l], prm['rw_r_k'][l], prm['rw_ln_w'][l], prm['rw_ln_b'][l])
    h_pool, pool_new = _pool(p_pool, pool_buf, pos0, prm['pool_w'][l], prm['pool_scale'][l])
    h_hg, hg_S_new = _hgrn2(p_hg, hg_S, lb_all[l], prm['hg_norm_w'][l])
    h_nsa, kv_new, win_cat = _nsa(p_nsa, kv_past, win_buf, pos0, prm['cmp_pe'][l], prm['cmp_w1'][l], prm['cmp_w2'][l])
    win_new = win_cat[:, -win_keep:]
    h = jnp.stack([h_rw, h_pool, h_hg, h_nsa]).reshape(N_BRANCH, m, BRANCH_W).astype(bf)
    merged = _gated_merge(h, w_branch_b[l], proj, 1024, 1024)
    y = _matmul(merged.astype(bf), w_out_b[l], 1024, 1024)
    x2 = _layer_norm(ALPHA * x2 + y, prm['ln_w'][l, 0], prm['ln_b'][l, 0])
    up = _matmul(x2.astype(bf), ffn_up_b[l], 1024, 512).reshape(B, T, 2 * D_FF)
    u, v = up[..., :D_FF], up[..., D_FF:]
    uc = jnp.concatenate([ffn_buf, u], axis=1)
    w_conv = prm['ffn_conv'][l]
    c = w_conv[0] * uc[:, 0:T]
    for j in range(1, CONV_W):
        c = c + w_conv[j] * uc[:, j:j + T]
    hcv = (jax.nn.gelu(c) * v).reshape(m, D_FF)
    f = _matmul_ksplit(hcv.astype(bf), ffn_down_b[l], 1024, 512, D_FF // 2)
    x2 = _layer_norm(ALPHA * x2 + f, prm['ln_w'][l, 1], prm['ln_b'][l, 1])
    ffn_new = uc[:, -(CONV_W - 1):]
    return x2.reshape(B, T, D_MODEL), (kv_new, win_new, rw_S_new, rw_shift_new, pool_new, hg_S_new, ffn_new)


def kernel(x_prompt, x_sample, cache_kv, page_table, state_win, state_rwkv, state_rwkv_shift, state_pool, state_hgrn, state_ffn_conv, w_in, rw_mu, rw_w0, rw_w2, rw_a0, rw_a2, rw_g2, rw_k_k, rw_k_a, rw_r_k, rw_ln_w, rw_ln_b, pool_w, pool_scale, hg_lb_raw, hg_norm_w, cmp_pe, cmp_w1, cmp_w2, w_branch, w_out, ln_w, ln_b, ffn_up, ffn_conv, ffn_down):
    prm = dict(rw_mu=rw_mu, rw_w0=rw_w0, rw_w2=rw_w2, rw_a0=rw_a0, rw_a2=rw_a2, rw_g2=rw_g2, rw_k_k=rw_k_k,
               rw_k_a=rw_k_a, rw_r_k=rw_r_k, rw_ln_w=rw_ln_w, rw_ln_b=rw_ln_b, pool_w=pool_w,
               pool_scale=pool_scale, hg_norm_w=hg_norm_w, cmp_pe=cmp_pe, cmp_w1=cmp_w1, cmp_w2=cmp_w2,
               ln_w=ln_w, ln_b=ln_b, ffn_conv=ffn_conv)
    lb_cum = jnp.cumsum(jax.nn.softmax(hg_lb_raw.astype(jnp.float32), axis=0), axis=0)
    lb_all = lb_cum - lb_cum[:1]
    past_len = page_table.shape[1] * cache_kv.shape[2]
    win_keep = state_win.shape[2]
    wts = _pack_weights(w_in, w_branch, w_out, ffn_up, ffn_down)

    bp = x_prompt.shape[0]
    dt = x_prompt.dtype
    z_win = jnp.zeros((bp, WINDOW, 2, NSA_G, NSA_HD), dt)
    z_rw = jnp.zeros((bp, RW_H, RW_HD, RW_HD), dt)
    z_shift = jnp.zeros((bp, RW_COLS), dt)
    z_pool = jnp.zeros((bp, POOL_BUF, BRANCH_W), dt)
    z_hg = jnp.zeros((bp, HG_H, HG_DK, HG_DV), dt)
    z_ffn = jnp.zeros((bp, CONV_W - 1, D_FF), dt)
    xp = x_prompt
    new_p = []
    for l in range(DEPTH):
        xp, st = _layer(l, xp, 0, None, z_win, z_rw, z_shift, z_pool, z_hg, z_ffn, wts, prm, lb_all, win_keep)
        new_p.append(st)

    db = x_sample.shape[0]
    xs = x_sample
    new_s = []
    for l in range(DEPTH):
        kv_past = cache_kv[l][page_table].reshape(db, past_len, 4, NSA_G, NSA_HD)
        xs, st = _layer(l, xs, past_len, kv_past, state_win[l], state_rwkv[l], state_rwkv_shift[l],
                        state_pool[l], state_hgrn[l], state_ffn_conv[l], wts, prm, lb_all, win_keep)
        new_s.append(st)

    sp = [jnp.stack([st[i] for st in new_p]) for i in range(7)]
    ss = [jnp.stack([st[i] for st in new_s]) for i in range(7)]
    return (xp, xs, sp[0], ss[0], sp[1], ss[1], sp[2], ss[2], sp[3], ss[3], sp[4], ss[4], sp[5], ss[5], sp[6], ss[6])
```

```python
import functools
import math

import jax
import jax.numpy as jnp
import numpy as np
from jax import lax
from jax.experimental import pallas as pl
from jax.experimental.pallas import tpu as pltpu

D_MODEL = 4096
DEPTH = 4
N_BRANCH = 4
BRANCH_W = D_MODEL // N_BRANCH
RW_HD = 64
RW_H = BRANCH_W // RW_HD
RW_W_RANK = 64
RW_A_RANK = 64
RW_G_RANK = 128
RW_COLS = 3 * BRANCH_W + RW_W_RANK + RW_A_RANK + RW_G_RANK
RW_GN_EPS = 64e-5
POOL_WINDOWS = (2, 4, 8, 16)
POOL_GW = BRANCH_W // len(POOL_WINDOWS)
POOL_BUF = max(POOL_WINDOWS) - 1
HG_DK = 128
HG_H = BRANCH_W // HG_DK
HG_DV = BRANCH_W // HG_H
HG_FDIM = HG_H * HG_DK
HG_COLS = 2 * HG_FDIM + 2 * BRANCH_W
HG_CHUNK = 64
HG_EPS = 1e-6
NSA_HD = 64
NSA_H = BRANCH_W // NSA_HD
NSA_G = 4
NSA_R = NSA_H // NSA_G
NSA_SCALE = NSA_HD ** -0.5
CMP_BLOCK = 32
CMP_STRIDE = 16
CMP_HIDDEN = 128
SEL_BLOCK = 64
N_SEL = 8
WINDOW = 512
Q_BLOCK = 128
NSA_KV_COLS = 6 * NSA_G * NSA_HD
NSA_COLS = BRANCH_W + NSA_KV_COLS + 3 * NSA_H
GATE_COLS = N_BRANCH * D_MODEL
IN_COLS = RW_COLS + BRANCH_W + HG_COLS + NSA_COLS + GATE_COLS
D_FF = 256 * ((8 * D_MODEL // 3 + 255) // 256)
CONV_W = 3
ALPHA = (2 * DEPTH) ** 0.25
LN_EPS = 1e-5
NEG = -1e30

MIX_COLS = RW_COLS + BRANCH_W + HG_COLS + NSA_COLS
IN_TN = 1024
MIX_PAD = -(-MIX_COLS // IN_TN) * IN_TN
IN_PACKED = MIX_PAD + GATE_COLS

VMEM_LIMIT = 56 * 1024 * 1024


def _mm_kernel(a_ref, b_ref, o_ref):
    o_ref[...] = jnp.dot(a_ref[...], b_ref[...], preferred_element_type=jnp.float32).astype(o_ref.dtype)


def _matmul(a, b, tm, tn, out_dtype=jnp.float32):
    m, k = a.shape
    _, n = b.shape
    tm = min(tm, m)
    assert m % tm == 0 and n % tn == 0, (m, n, tm, tn)
    return pl.pallas_call(
        _mm_kernel,
        grid=(m // tm, n // tn),
        in_specs=[pl.BlockSpec((tm, k), lambda i, j: (i, 0)),
                  pl.BlockSpec((k, tn), lambda i, j: (0, j))],
        out_specs=pl.BlockSpec((tm, tn), lambda i, j: (i, j)),
        out_shape=jax.ShapeDtypeStruct((m, n), out_dtype),
        compiler_params=pltpu.CompilerParams(
            dimension_semantics=("arbitrary", "arbitrary"), vmem_limit_bytes=VMEM_LIMIT),
        name="matmul",
    )(a, b)


def _mm_ksplit_kernel(a_ref, b_ref, o_ref):
    @pl.when(pl.program_id(2) == 0)
    def _():
        o_ref[...] = jnp.zeros_like(o_ref)

    o_ref[...] += jnp.dot(a_ref[...], b_ref[...], preferred_element_type=jnp.float32)


def _matmul_ksplit(a, b, tm, tn, tk):
    m, k = a.shape
    _, n = b.shape
    tm = min(tm, m)
    assert m % tm == 0 and n % tn == 0 and k % tk == 0
    return pl.pallas_call(
        _mm_ksplit_kernel,
        grid=(m // tm, n // tn, k // tk),
        in_specs=[pl.BlockSpec((tm, tk), lambda i, j, kk: (i, kk)),
                  pl.BlockSpec((tk, tn), lambda i, j, kk: (kk, j))],
        out_specs=pl.BlockSpec((tm, tn), lambda i, j, kk: (i, j)),
        out_shape=jax.ShapeDtypeStruct((m, n), jnp.float32),
        compiler_params=pltpu.CompilerParams(
            dimension_semantics=("arbitrary", "arbitrary", "arbitrary"), vmem_limit_bytes=VMEM_LIMIT),
        name="matmul_ksplit",
    )(a, b)


def _merge_kernel(h_ref, w_ref, g_ref, o_ref):
    n = pl.program_id(2)

    @pl.when(n == 0)
    def _():
        o_ref[...] = jnp.zeros_like(o_ref)

    y = jnp.dot(h_ref[0], w_ref[0], preferred_element_type=jnp.float32)
    o_ref[...] += jax.nn.sigmoid(g_ref[...]) * y


def _gated_merge(h, w, proj, tm, tn):
    nb, m, bw = h.shape
    d = w.shape[-1]
    tm = min(tm, m)
    goff = MIX_PAD // tn
    nj = d // tn
    return pl.pallas_call(
        _merge_kernel,
        grid=(m // tm, nj, nb),
        in_specs=[pl.BlockSpec((1, tm, bw), lambda i, j, n: (n, i, 0)),
                  pl.BlockSpec((1, bw, tn), lambda i, j, n: (n, 0, j)),
                  pl.BlockSpec((tm, tn), lambda i, j, n: (i, goff + n * nj + j))],
        out_specs=pl.BlockSpec((tm, tn), lambda i, j, n: (i, j)),
        out_shape=jax.ShapeDtypeStruct((m, d), jnp.float32),
        compiler_params=pltpu.CompilerParams(
            dimension_semantics=("arbitrary", "arbitrary", "arbitrary"), vmem_limit_bytes=VMEM_LIMIT),
        name="gated_merge",
    )(h, w, proj)


LN_ROWS = 256


def _ln_kernel(x_ref, y_ref, w_ref, b_ref, o_ref, ob_ref):
    z = ALPHA * x_ref[...] + y_ref[...]
    mu = jnp.mean(z, axis=-1, keepdims=True)
    zc = z - mu
    var = jnp.mean(zc * zc, axis=-1, keepdims=True)
    o = zc * lax.rsqrt(var + LN_EPS) * w_ref[...] + b_ref[...]
    o_ref[...] = o
    ob_ref[...] = o.astype(ob_ref.dtype)


def _residual_layer_norm(x, y, w, b):
    m, d = x.shape
    tr = min(LN_ROWS, m)
    assert m % tr == 0
    row = pl.BlockSpec((tr, d), lambda i: (i, 0))
    vec = pl.BlockSpec((1, d), lambda i: (0, 0))
    return pl.pallas_call(
        _ln_kernel,
        grid=(m // tr,),
        in_specs=[row, row, vec, vec],
        out_specs=[row, row],
        out_shape=[jax.ShapeDtypeStruct((m, d), jnp.float32), jax.ShapeDtypeStruct((m, d), jnp.bfloat16)],
        compiler_params=pltpu.CompilerParams(dimension_semantics=("arbitrary",), vmem_limit_bytes=VMEM_LIMIT),
        name="residual_layer_norm",
    )(x, y, w.reshape(1, d), b.reshape(1, d))


FFN_ACT_COLS = D_FF // 2


def _ffn_act_kernel(u_ref, halo_ref, buf_ref, v_ref, wc_ref, o_ref, *, blocks_per_seq):
    i = pl.program_id(1)
    u = u_ref[...]
    tr = u.shape[0]
    first = (i % blocks_per_seq) == 0
    halo = halo_ref[...]
    prev = jnp.where(first, buf_ref[0], halo[SUBLANES - (CONV_W - 1):, :])
    ridx = lax.broadcasted_iota(jnp.int32, u.shape, 0)
    c = wc_ref[CONV_W - 1:CONV_W, :] * u
    for back in range(1, CONV_W):
        sh = pltpu.roll(u, back, axis=0)
        for j in range(back):
            sh = jnp.where(ridx == j, prev[CONV_W - 1 - back + j:CONV_W - back + j, :], sh)
        c = c + wc_ref[CONV_W - 1 - back:CONV_W - back, :] * sh
    o_ref[...] = (jax.nn.gelu(c) * v_ref[...]).astype(o_ref.dtype)


def _ffn_activation(up, buf, w_conv, seq_len, tr):
    m = up.shape[0]
    tr = min(tr, seq_len)
    tc = FFN_ACT_COLS
    ncb = D_FF // tc
    assert seq_len % tr == 0 and tr % SUBLANES == 0 and m % seq_len == 0
    bps = seq_len // tr
    hb = tr // SUBLANES
    return pl.pallas_call(
        functools.partial(_ffn_act_kernel, blocks_per_seq=bps),
        grid=(ncb, m // tr),
        in_specs=[pl.BlockSpec((tr, tc), lambda j, i: (i, j)),
                  pl.BlockSpec((SUBLANES, tc), lambda j, i: (jnp.maximum(i * hb - 1, 0), j)),
                  pl.BlockSpec((1, CONV_W - 1, tc), lambda j, i: (i // bps, 0, j)),
                  pl.BlockSpec((tr, tc), lambda j, i: (i, ncb + j)),
                  pl.BlockSpec((CONV_W, tc), lambda j, i: (0, j))],
        out_specs=pl.BlockSpec((tr, tc), lambda j, i: (i, j)),
        out_shape=jax.ShapeDtypeStruct((m, D_FF), jnp.bfloat16),
        compiler_params=pltpu.CompilerParams(
            dimension_semantics=("arbitrary", "arbitrary"), vmem_limit_bytes=VMEM_LIMIT),
        name="ffn_activation",
    )(up, up, buf, up, w_conv)


LANES = 128
SUBLANES = 8
RW_TB = 64
RW_NK = 5


def _rw_rec_kernel(kx_ref, v_ref, s0_ref, y_ref, st_ref, s_scr, *, tb, vs):
    i = pl.program_id(0)

    @pl.when(i == 0)
    def _():
        s_scr[...] = s0_ref[...]

    def step(t, carry):
        def group(gi, c2):
            base = pl.multiple_of(gi * SUBLANES, SUBLANES)
            vrows = v_ref[t, pl.ds(base, SUBLANES), :]
            ys = []
            for j in range(SUBLANES):
                s = s_scr[base + j]
                sa = jnp.sum(s * kx_ref[t, 0], axis=0, keepdims=True)
                s = s * kx_ref[t, 2] + sa * kx_ref[t, 1] + vrows[j:j + 1, :] * kx_ref[t, 3]
                s_scr[base + j] = s
                ys.append(jnp.sum(s * kx_ref[t, 4], axis=0, keepdims=True))
            y_ref[t, pl.ds(base, SUBLANES), :] = jnp.concatenate(ys, axis=0)
            return c2

        lax.fori_loop(0, vs // SUBLANES, group, 0)
        return carry

    lax.fori_loop(0, tb, step, 0)

    @pl.when(i == pl.num_programs(0) - 1)
    def _():
        st_ref[...] = s_scr[...]


def _rwkv_recurrence(rh, k2h, vh, kkh, ah, dh, s0):
    b, t, h, hd = rh.shape
    fold = LANES // (b * h)
    vs = hd // fold
    tb = min(RW_TB, t)
    assert fold * b * h == LANES and t % tb == 0 and vs % SUBLANES == 0
    kx = jnp.stack([-kkh, kkh * ah, dh, k2h, rh])
    kx = kx.transpose(2, 0, 4, 1, 3).reshape(t, RW_NK, hd, b * h)
    kx = jnp.tile(kx, (1, 1, 1, fold))
    vv = vh.reshape(b, t, h, fold, vs).transpose(1, 4, 3, 0, 2).reshape(t, vs, LANES)
    s0l = s0.reshape(b, h, fold, vs, hd).transpose(3, 4, 2, 0, 1).reshape(vs, hd, LANES)
    y, st = pl.pallas_call(
        functools.partial(_rw_rec_kernel, tb=tb, vs=vs),
        grid=(t // tb,),
        in_specs=[pl.BlockSpec((tb, RW_NK, hd, LANES), lambda i: (i, 0, 0, 0)),
                  pl.BlockSpec((tb, vs, LANES), lambda i: (i, 0, 0)),
                  pl.BlockSpec((vs, hd, LANES), lambda i: (0, 0, 0))],
        out_specs=[pl.BlockSpec((tb, vs, LANES), lambda i: (i, 0, 0)),
                   pl.BlockSpec((vs, hd, LANES), lambda i: (0, 0, 0))],
        out_shape=[jax.ShapeDtypeStruct((t, vs, LANES), jnp.float32),
                   jax.ShapeDtypeStruct((vs, hd, LANES), jnp.float32)],
        scratch_shapes=[pltpu.VMEM((vs, hd, LANES), jnp.float32)],
        compiler_params=pltpu.CompilerParams(
            dimension_semantics=("arbitrary",), vmem_limit_bytes=VMEM_LIMIT),
        name="rwkv_recurrence",
    )(kx, vv, s0l)
    y = y.reshape(t, vs, fold, b, h).transpose(3, 0, 4, 2, 1).reshape(b, t, h, hd)
    st = st.reshape(vs, hd, fold, b, h).transpose(3, 4, 2, 0, 1).reshape(b, h, hd, hd)
    return y, st


def _nsa_kernel(q_ref, kc_ref, vc_ref, ks_ref, vs_ref, kw_ref, vw_ref, gl_ref, ov_ref, e_ref, o_ref,
                *, n_keys, wb):
    f32 = jnp.float32
    cdt = q_ref.dtype
    qb = Q_BLOCK
    qi = pl.program_id(2)
    nt = (((1,), (1,)), ((), ()))
    tq = lax.broadcasted_iota(jnp.int32, (qb, 1), 0) + qi * qb

    def softmax_parts(s, mask):
        s = jnp.where(mask, s, NEG)
        e = jnp.exp(s - jnp.max(s, axis=-1, keepdims=True))
        return e, jnp.sum(e, axis=-1, keepdims=True)

    cidx = lax.broadcasted_iota(jnp.int32, (qb, LANES), 1)
    maskc = (cidx * CMP_STRIDE + (CMP_BLOCK - 1)) <= tq
    kc = kc_ref[0, 0]
    vc = vc_ref[0, 0]
    psum = jnp.zeros((qb, LANES), f32)
    o_cmp = []
    for r in range(NSA_R):
        q_r = q_ref[0, 0, 0, r * qb:(r + 1) * qb, :]
        s = lax.dot_general(q_r, kc, nt, preferred_element_type=f32) * NSA_SCALE
        e, l = softmax_parts(s, maskc)
        pr = jnp.where(maskc, e / l, 0.0)
        psum = psum + pr
        o_cmp.append(jnp.dot(pr.astype(cdt), vc, preferred_element_type=f32))

    imp = jnp.dot(psum, ov_ref[...], preferred_element_type=f32, precision=lax.Precision.HIGHEST)
    cur = lax.shift_right_logical(tq, int(math.log2(SEL_BLOCK)))
    sc = jnp.where(cidx <= cur, imp, -jnp.inf)
    sc = jnp.where(cidx == cur, jnp.inf, sc)
    ns = n_keys // SEL_BLOCK
    sct = sc.T[:ns]
    sidx = lax.broadcasted_iota(jnp.int32, (ns, qb), 0)
    rank = jnp.zeros((ns, qb), jnp.int32)
    for sp in range(ns):
        row = sct[sp:sp + 1, :]
        beats = (row > sct) | ((row == sct) & (sidx > sp))
        rank = rank + beats.astype(jnp.int32)
    selt = (rank < N_SEL).astype(f32)
    if ns < LANES:
        selt = jnp.concatenate([selt, jnp.zeros((LANES - ns, qb), f32)], axis=0)
    msel = jnp.dot(selt.T.astype(jnp.bfloat16), e_ref[...], preferred_element_type=f32)
    kpos = lax.broadcasted_iota(jnp.int32, (qb, n_keys), 1)
    valid = (msel > 0.5) & (kpos <= tq)

    kposw = lax.broadcasted_iota(jnp.int32, (qb, wb + qb), 1) + (qi * qb - wb)
    rel = tq - kposw
    maskw = (rel >= 0) & (rel < WINDOW) & (kposw >= 0)
    wstart = pl.multiple_of(qi * qb, qb)
    kw = kw_ref[0, 0, pl.ds(wstart, wb + qb), :]
    vw = vw_ref[0, 0, pl.ds(wstart, wb + qb), :]

    ks = ks_ref[0, 0]
    vs = vs_ref[0, 0]
    outs = []
    for r in range(NSA_R):
        q_r = q_ref[0, 0, 0, r * qb:(r + 1) * qb, :]
        s = lax.dot_general(q_r, ks, nt, preferred_element_type=f32) * NSA_SCALE
        e, l = softmax_parts(s, valid)
        o_sel = jnp.dot(e.astype(cdt), vs, preferred_element_type=f32) / l
        s = lax.dot_general(q_r, kw, nt, preferred_element_type=f32) * NSA_SCALE
        e, l = softmax_parts(s, maskw)
        o_win = jnp.dot(e.astype(cdt), vw, preferred_element_type=f32) / l
        g = jax.nn.sigmoid(gl_ref[0, 0, 0, r * qb:(r + 1) * qb, :])
        outs.append(g[:, 0:1] * o_cmp[r] + g[:, 1:2] * o_sel + g[:, 2:3] * o_win)
    o_ref[...] = jnp.concatenate(outs, axis=1).astype(o_ref.dtype)


def _nsa_prompt(q, kc, vc, kv6, gl, win_buf, cdt, out_dtype):
    b, t, _ = q.shape
    qb = Q_BLOCK
    nqb = t // qb
    wb = win_buf.shape[1]
    nc = kc.shape[1]
    assert t % qb == 0 and nc <= LANES and t // SEL_BLOCK <= LANES and t // SEL_BLOCK >= N_SEL
    qa = q.reshape(b, nqb, qb, NSA_G, NSA_R, NSA_HD).transpose(0, 3, 1, 4, 2, 5)
    qa = qa.reshape(b, NSA_G, nqb, NSA_R * qb, NSA_HD).astype(cdt)
    gla = gl.reshape(b, nqb, qb, 3, NSA_G, NSA_R).transpose(0, 4, 1, 5, 2, 3).reshape(b, NSA_G, nqb, NSA_R * qb, 3)

    def bg(x):
        return x.transpose(0, 2, 1, 3).astype(cdt)

    pad = ((0, 0), (0, LANES - nc), (0, 0), (0, 0))
    kca, vca = bg(jnp.pad(kc, pad)), bg(jnp.pad(vc, pad))
    ksa, vsa = bg(kv6[:, :, 2]), bg(kv6[:, :, 3])
    kwa = bg(jnp.concatenate([win_buf[:, :, 0], kv6[:, :, 4]], axis=1))
    vwa = bg(jnp.concatenate([win_buf[:, :, 1], kv6[:, :, 5]], axis=1))
    ns = t // SEL_BLOCK
    ov = np.zeros((LANES, LANES), np.float32)
    cs = np.arange(nc)[:, None] * CMP_STRIDE
    ss = np.arange(ns)[None, :] * SEL_BLOCK
    ov[:nc, :ns] = (cs < ss + SEL_BLOCK) & (cs + CMP_BLOCK > ss)
    e = (np.arange(LANES)[:, None] == (np.arange(t)[None, :] // SEL_BLOCK)).astype(np.float32)
    kvspec = lambda n: pl.BlockSpec((1, 1, n, NSA_HD), lambda bi, gi, qi: (bi, gi, 0, 0))
    return pl.pallas_call(
        functools.partial(_nsa_kernel, n_keys=t, wb=wb),
        grid=(b, NSA_G, nqb),
        in_specs=[pl.BlockSpec((1, 1, 1, NSA_R * qb, NSA_HD), lambda bi, gi, qi: (bi, gi, qi, 0, 0)),
                  kvspec(LANES), kvspec(LANES), kvspec(t), kvspec(t), kvspec(wb + t), kvspec(wb + t),
                  pl.BlockSpec((1, 1, 1, NSA_R * qb, 3), lambda bi, gi, qi: (bi, gi, qi, 0, 0)),
                  pl.BlockSpec((LANES, LANES), lambda bi, gi, qi: (0, 0)),
                  pl.BlockSpec((LANES, t), lambda bi, gi, qi: (0, 0))],
        out_specs=pl.BlockSpec((qb, NSA_R * NSA_HD), lambda bi, gi, qi: (bi * nqb + qi, gi)),
        out_shape=jax.ShapeDtypeStruct((b * t, BRANCH_W), out_dtype),
        compiler_params=pltpu.CompilerParams(
            dimension_semantics=("arbitrary", "arbitrary", "arbitrary"), vmem_limit_bytes=VMEM_LIMIT),
        name="nsa_prompt",
    )(qa, kca, vca, ksa, vsa, kwa, vwa, gla, jnp.asarray(ov), jnp.asarray(e, jnp.bfloat16))


def _split(x, sizes):
    idx = tuple(int(s) for s in np.cumsum(sizes)[:-1])
    return jnp.split(x, idx, axis=-1)


def _layer_norm(x, w, b):
    mu = x.mean(-1, keepdims=True)
    var = jnp.mean(jnp.square(x - mu), -1, keepdims=True)
    return (x - mu) * lax.rsqrt(var + LN_EPS) * w + b


def _rwkv7(p, shift_prev, S0, mu, w0, w2, a0, a2, g2, k_k, k_a, r_k, ln_w, ln_b):
    B, T, _ = p.shape
    f32 = jnp.float32
    p_prev = jnp.concatenate([shift_prev[:, None], p[:, :-1]], axis=1)
    xs = p + mu * (p_prev - p)
    r, k, v, wl, al, gl = _split(xs, (BRANCH_W, BRANCH_W, BRANCH_W, RW_W_RANK, RW_A_RANK, RW_G_RANK))
    w = -jax.nn.softplus(-(w0 + jnp.tanh(wl) @ w2).astype(f32)) - 0.5
    decay = jnp.exp(-jnp.exp(w))
    a = jax.nn.sigmoid((a0 + al @ a2).astype(f32))
    g = jax.nn.sigmoid(gl) @ g2

    def heads(t):
        return t.astype(f32).reshape(B, T, RW_H, RW_HD)

    kk = heads(k * k_k)
    kk = kk / jnp.maximum(jnp.linalg.norm(kk, axis=-1, keepdims=True), 1e-12)
    k2 = k.astype(f32) * (1.0 + (a - 1.0) * k_a)
    rh, kh, vh, ah, dh = heads(r), heads(k2), heads(v), heads(a), heads(decay)

    y, S_T = _rwkv_recurrence(rh, kh, vh, kk, ah, dh, S0.astype(f32))
    mean = y.mean(-1, keepdims=True)
    var = jnp.mean(jnp.square(y - mean), -1, keepdims=True)
    yn = ((y - mean) * lax.rsqrt(var + RW_GN_EPS)).reshape(B, T, BRANCH_W) * ln_w + ln_b
    bonus = (jnp.sum(rh * kh * r_k, -1, keepdims=True) * vh).reshape(B, T, BRANCH_W)
    out = (yn + bonus) * g
    return out, S_T, p[:, -1]


def _pool(p, buf, pos0, w_grp, scale):
    B, T, _ = p.shape
    xc = jnp.concatenate([buf, p], axis=1)
    cs = jnp.cumsum(xc.astype(jnp.float32), axis=1)
    cs = jnp.pad(cs, ((0, 0), (1, 0), (0, 0)))
    end = cs[:, POOL_BUF + 1: POOL_BUF + 1 + T]
    t = jnp.arange(T)
    outs = []
    for gi, w in enumerate(POOL_WINDOWS):
        lo, hi = gi * POOL_GW, (gi + 1) * POOL_GW
        start = cs[:, POOL_BUF + 1 - w: POOL_BUF + 1 - w + T, lo:hi]
        cnt = jnp.minimum(w, pos0 + t + 1).astype(jnp.float32)[None, :, None]
        outs.append((end[..., lo:hi] - start) / cnt - p[..., lo:hi].astype(jnp.float32))
    d = jnp.stack(outs, axis=2)
    y = jnp.einsum('btgc,gcd->btgd', d.astype(p.dtype), w_grp).reshape(B, T, BRANCH_W) * scale
    return y, xc[:, -POOL_BUF:]


def _hgrn2(p, S0, lb, norm_w):
    B, T, _ = p.shape
    f32 = jnp.float32
    q, f, i, g = _split(p, (HG_FDIM, HG_FDIM, BRANCH_W, BRANCH_W))
    ff = f.astype(f32).reshape(B, T, HG_H, HG_DK)
    lbh = lb.astype(f32).reshape(HG_H, HG_DK)
    log_f = jnp.logaddexp(jnp.log(lbh), jnp.log1p(-lbh) + jax.nn.log_sigmoid(ff))
    kk = (1.0 - lbh) * jax.nn.sigmoid(-ff)
    qq = jax.nn.silu(q.astype(f32)).reshape(B, T, HG_H, HG_DK)
    vv = i.astype(f32).reshape(B, T, HG_H, HG_DV)
    C = min(HG_CHUNK, T)
    nc = -(-T // C)
    pad = nc * C - T

    def chunks(t):
        t = jnp.pad(t, ((0, 0), (0, pad), (0, 0), (0, 0)))
        return t.reshape(B, nc, C, HG_H, t.shape[-1]).transpose(1, 0, 3, 2, 4)

    tri = jnp.tril(jnp.ones((C, C), dtype=bool))[:, :, None]

    def step(S, inp):
        qc, kc, vc, gc = inp
        b = jnp.cumsum(gc, axis=2)
        inter = jnp.einsum('bhck,bhkv->bhcv', qc * jnp.exp(b), S)
        diff = b[:, :, :, None, :] - b[:, :, None, :, :]
        dec = jnp.where(tri, jnp.exp(jnp.where(tri, diff, 0.0)), 0.0)
        att = jnp.einsum('bhtk,bhtsk,bhsk->bhts', qc, dec, kc)
        o = inter + jnp.einsum('bhts,bhsv->bhtv', att, vc)
        bl = b[:, :, -1]
        S = S * jnp.exp(bl)[..., None] + jnp.einsum('bhck,bhcv->bhkv', kc * jnp.exp(bl[:, :, None] - b), vc)
        return S, o

    S_T, o = lax.scan(step, S0.astype(f32), (chunks(qq), chunks(kk), chunks(vv), chunks(log_f)))
    o = o.transpose(1, 0, 3, 2, 4).reshape(B, nc * C, HG_H, HG_DV)[:, :T]
    o = o * lax.rsqrt(jnp.mean(jnp.square(o), -1, keepdims=True) + HG_EPS)
    o = o.reshape(B, T, BRANCH_W) * norm_w * jax.nn.silu(g.astype(f32))
    return o, S_T


def _compress(kv, pe, w1, w2):
    B, N = kv.shape[:2]
    nc = (N - CMP_BLOCK) // CMP_STRIDE + 1
    idx = jnp.arange(nc)[:, None] * CMP_STRIDE + jnp.arange(CMP_BLOCK)[None, :]
    blocks = kv[:, idx] + pe[None, None, :, None, :]
    blocks = blocks.transpose(0, 1, 3, 2, 4).reshape(B, nc, NSA_G, CMP_BLOCK * NSA_HD)
    return jax.nn.gelu(blocks @ w1) @ w2


def _overlap(nc, ns):
    cs = jnp.arange(nc)[:, None] * CMP_STRIDE
    ss = jnp.arange(ns)[None, :] * SEL_BLOCK
    return ((cs < ss + SEL_BLOCK) & (cs + CMP_BLOCK > ss)).astype(jnp.float32)


def _cmp_attention(qg, kc, vc, pos0):
    T = qg.shape[1]
    nc = kc.shape[1]
    s = jnp.einsum('btgrd,bcgd->bgrtc', qg, kc).astype(jnp.float32) * NSA_SCALE
    qpos = pos0 + jnp.arange(T)
    bend = jnp.arange(nc) * CMP_STRIDE + CMP_BLOCK - 1
    mask = bend[None, :] <= qpos[:, None]
    pr = jax.nn.softmax(jnp.where(mask, s, NEG), axis=-1) * mask
    o = jnp.einsum('bgrtc,bcgd->btgrd', pr.astype(vc.dtype), vc)
    return o, pr


def _sel_attention(qg, k, v, imp, pos0):
    B, T = qg.shape[:2]
    N = k.shape[1]
    ns = imp.shape[-1]
    n_sel = min(N_SEL, ns)
    padn = ns * SEL_BLOCK - N
    kb = jnp.pad(k, ((0, 0), (0, padn), (0, 0), (0, 0))).reshape(B, ns, SEL_BLOCK, NSA_G, NSA_HD).transpose(0, 3, 1, 2, 4)
    vb = jnp.pad(v, ((0, 0), (0, padn), (0, 0), (0, 0))).reshape(B, ns, SEL_BLOCK, NSA_G, NSA_HD).transpose(0, 3, 1, 2, 4)
    qpos = pos0 + jnp.arange(T)
    cur = qpos // SEL_BLOCK
    blk = jnp.arange(ns)
    sc = jnp.where(blk[None, :] <= cur[:, None], imp, -jnp.inf)
    sc = jnp.where(blk[None, :] == cur[:, None], jnp.inf, sc)
    _, idx = lax.top_k(sc, n_sel)
    Qb = min(Q_BLOCK, T)
    nqb = -(-T // Qb)
    Tp = nqb * Qb
    q_all = jnp.pad(qg, ((0, 0), (0, Tp - T), (0, 0), (0, 0), (0, 0))).reshape(B, nqb, Qb, NSA_G, NSA_R, NSA_HD).transpose(1, 0, 2, 3, 4, 5)
    idx_all = jnp.pad(idx, ((0, 0), (0, 0), (0, Tp - T), (0, 0))).reshape(B, NSA_G, nqb, Qb, n_sel).transpose(2, 0, 1, 3, 4)
    pos_all = (pos0 + jnp.arange(Tp)).reshape(nqb, Qb)
    bi = jnp.arange(B)[:, None, None, None]
    gi = jnp.arange(NSA_G)[None, :, None, None]
    lpos = jnp.arange(SEL_BLOCK)

    def block(inp):
        qb, ib, pb = inp
        kg = kb[bi, gi, ib]
        vg = vb[bi, gi, ib]
        s = jnp.einsum('bqgrd,bgqnld->bgrqnl', qb, kg).astype(jnp.float32) * NSA_SCALE
        kpos = ib[..., None] * SEL_BLOCK + lpos
        valid = kpos <= pb[None, None, :, None, None]
        s = jnp.where(valid[:, :, None], s, NEG)
        pr = jax.nn.softmax(s.reshape(B, NSA_G, NSA_R, Qb, n_sel * SEL_BLOCK), axis=-1).reshape(s.shape)
        return jnp.einsum('bgrqnl,bgqnld->bqgrd', pr.astype(vg.dtype), vg)

    o = lax.map(block, (q_all, idx_all, pos_all))
    return jnp.moveaxis(o, 0, 1).reshape(B, Tp, NSA_G, NSA_R, NSA_HD)[:, :T]


def _win_attention(qg, kv_new, kv_buf, pos0):
    B, T = qg.shape[:2]
    wb = kv_buf.shape[1]
    kv_cat = jnp.concatenate([kv_buf, kv_new], axis=1)
    Qb = min(Q_BLOCK, T)
    nqb = -(-T // Qb)
    Tp = nqb * Qb
    kv_pad = jnp.pad(kv_cat, ((0, 0), (0, Tp - T), (0, 0), (0, 0), (0, 0)))
    q_pad = jnp.pad(qg, ((0, 0), (0, Tp - T), (0, 0), (0, 0), (0, 0)))

    def block(c):
        start = c * Qb
        qs = lax.dynamic_slice_in_dim(q_pad, start, Qb, axis=1)
        kvs = lax.dynamic_slice_in_dim(kv_pad, start, wb + Qb, axis=1)
        qpos = pos0 + start + jnp.arange(Qb)
        kpos = pos0 - wb + start + jnp.arange(wb + Qb)
        rel = qpos[:, None] - kpos[None, :]
        mask = (rel >= 0) & (rel < WINDOW) & (kpos[None, :] >= 0)
        s = jnp.einsum('bqgrd,bkgd->bgrqk', qs, kvs[:, :, 0]).astype(jnp.float32) * NSA_SCALE
        pr = jax.nn.softmax(jnp.where(mask, s, NEG), axis=-1)
        return jnp.einsum('bgrqk,bkgd->bqgrd', pr.astype(kvs.dtype), kvs[:, :, 1])

    o = lax.map(block, jnp.arange(nqb))
    o = jnp.moveaxis(o, 0, 1).reshape(B, Tp, NSA_G, NSA_R, NSA_HD)[:, :T]
    return o, kv_cat


def _nsa(p, kv_past, win_buf, pos0, pe, w1, w2):
    B, T, _ = p.shape
    q, kv6, gl = _split(p, (BRANCH_W, NSA_KV_COLS, 3 * NSA_H))
    qg = q.reshape(B, T, NSA_G, NSA_R, NSA_HD)
    kv6 = kv6.reshape(B, T, 6, NSA_G, NSA_HD)
    kv_new = kv6[:, :, :4]
    kv_all = kv_new if kv_past is None else jnp.concatenate([kv_past, kv_new], axis=1)
    kc = _compress(kv_all[:, :, 0], pe[0], w1[0], w2[0])
    vc = _compress(kv_all[:, :, 1], pe[1], w1[1], w2[1])
    o_cmp, p_cmp = _cmp_attention(qg, kc, vc, pos0)
    ns = -(-kv_all.shape[1] // SEL_BLOCK)
    imp = jnp.einsum('bgrtc,cs->bgts', p_cmp, _overlap(kc.shape[1], ns))
    o_sel = _sel_attention(qg, kv_all[:, :, 2], kv_all[:, :, 3], imp, pos0)
    o_win, win_cat = _win_attention(qg, kv6[:, :, 4:], win_buf, pos0)
    g = jax.nn.sigmoid(gl.reshape(B, T, 3, NSA_G, NSA_R))[..., None]
    o = g[:, :, 0] * o_cmp + g[:, :, 1] * o_sel + g[:, :, 2] * o_win
    return o.reshape(B, T, BRANCH_W), kv_new, win_cat


def _nsa_nopast(p, win_buf, pe, w1, w2, cdt=jnp.bfloat16, out_dtype=jnp.bfloat16):
    B, T, _ = p.shape
    q, kv6, gl = _split(p, (BRANCH_W, NSA_KV_COLS, 3 * NSA_H))
    kv6 = kv6.reshape(B, T, 6, NSA_G, NSA_HD)
    kv_new = kv6[:, :, :4]
    kc = _compress(kv6[:, :, 0], pe[0], w1[0], w2[0])
    vc = _compress(kv6[:, :, 1], pe[1], w1[1], w2[1])
    o = _nsa_prompt(q, kc, vc, kv6, gl, win_buf, cdt, out_dtype)
    win_cat = jnp.concatenate([win_buf, kv6[:, :, 4:]], axis=1)
    return o, kv_new, win_cat


def _pack_weights(w_in, w_branch, w_out, ffn_up, ffn_down):
    bf = jnp.bfloat16
    w_in_p = jnp.concatenate(
        [w_in[:, :, :MIX_COLS].astype(bf),
         jnp.zeros((DEPTH, D_MODEL, MIX_PAD - MIX_COLS), bf),
         w_in[:, :, MIX_COLS:].astype(bf)], axis=-1)
    return w_in_p, w_branch.astype(bf), w_out.astype(bf), ffn_up.astype(bf), ffn_down.astype(bf)


def _layer(l, xpair, pos0, kv_past, win_buf, rw_S, rw_shift, pool_buf, hg_S, ffn_buf, wts, prm, lb_all, win_keep):
    (w_in_p, w_branch_b, w_out_b, ffn_up_b, ffn_down_b) = wts
    x, x2b = xpair
    B, T, _ = x.shape
    m = B * T
    bf = jnp.bfloat16
    x2 = x.reshape(m, D_MODEL)
    proj = _matmul(x2b, w_in_p[l], 1024, IN_TN)
    pm = proj[:, :MIX_COLS].reshape(B, T, MIX_COLS)
    p_rw, p_pool, p_hg, p_nsa = _split(pm, (RW_COLS, BRANCH_W, HG_COLS, NSA_COLS))
    h_rw, rw_S_new, rw_shift_new = _rwkv7(p_rw, rw_shift, rw_S, prm['rw_mu'][l], prm['rw_w0'][l], prm['rw_w2'][l],
                                          prm['rw_a0'][l], prm['rw_a2'][l], prm['rw_g2'][l], prm['rw_k_k'][l],
                                          prm['rw_k_a'][l], prm['rw_r_k'][l], prm['rw_ln_w'][l], prm['rw_ln_b'][l])
    h_pool, pool_new = _pool(p_pool, pool_buf, pos0, prm['pool_w'][l], prm['pool_scale'][l])
    h_hg, hg_S_new = _hgrn2(p_hg, hg_S, lb_all[l], prm['hg_norm_w'][l])
    if kv_past is None:
        h_nsa, kv_new, win_cat = _nsa_nopast(p_nsa, win_buf, prm['cmp_pe'][l], prm['cmp_w1'][l], prm['cmp_w2'][l])
    else:
        h_nsa, kv_new, win_cat = _nsa(p_nsa, kv_past, win_buf, pos0, prm['cmp_pe'][l], prm['cmp_w1'][l],
                                      prm['cmp_w2'][l])
    win_new = win_cat[:, -win_keep:]
    h = jnp.stack([t.reshape(m, BRANCH_W).astype(bf) for t in (h_rw, h_pool, h_hg, h_nsa)])
    merged = _gated_merge(h, w_branch_b[l], proj, 1024, 1024)
    y = _matmul(merged.astype(bf), w_out_b[l], 1024, 1024)
    x2, x2b = _residual_layer_norm(x2, y, prm['ln_w'][l, 0], prm['ln_b'][l, 0])
    up = _matmul(x2b, ffn_up_b[l], 1024, 512)
    hcv = _ffn_activation(up, ffn_buf, prm['ffn_conv'][l], T, 128)
    f = _matmul_ksplit(hcv, ffn_down_b[l], 1024, 512, D_FF // 2)
    x2, x2b = _residual_layer_norm(x2, f, prm['ln_w'][l, 1], prm['ln_b'][l, 1])
    assert T >= CONV_W - 1
    ffn_new = up.reshape(B, T, 2 * D_FF)[:, T - (CONV_W - 1):, :D_FF]
    return (x2.reshape(B, T, D_MODEL), x2b), (kv_new, win_new, rw_S_new, rw_shift_new, pool_new, hg_S_new, ffn_new)


def kernel(x_prompt, x_sample, cache_kv, page_table, state_win, state_rwkv, state_rwkv_shift, state_pool, state_hgrn, state_ffn_conv, w_in, rw_mu, rw_w0, rw_w2, rw_a0, rw_a2, rw_g2, rw_k_k, rw_k_a, rw_r_k, rw_ln_w, rw_ln_b, pool_w, pool_scale, hg_lb_raw, hg_norm_w, cmp_pe, cmp_w1, cmp_w2, w_branch, w_out, ln_w, ln_b, ffn_up, ffn_conv, ffn_down):
    prm = dict(rw_mu=rw_mu, rw_w0=rw_w0, rw_w2=rw_w2, rw_a0=rw_a0, rw_a2=rw_a2, rw_g2=rw_g2, rw_k_k=rw_k_k,
               rw_k_a=rw_k_a, rw_r_k=rw_r_k, rw_ln_w=rw_ln_w, rw_ln_b=rw_ln_b, pool_w=pool_w,
               pool_scale=pool_scale, hg_norm_w=hg_norm_w, cmp_pe=cmp_pe, cmp_w1=cmp_w1, cmp_w2=cmp_w2,
               ln_w=ln_w, ln_b=ln_b, ffn_conv=ffn_conv)
    lb_cum = jnp.cumsum(jax.nn.softmax(hg_lb_raw.astype(jnp.float32), axis=0), axis=0)
    lb_all = lb_cum - lb_cum[:1]
    past_len = page_table.shape[1] * cache_kv.shape[2]
    win_keep = state_win.shape[2]
    wts = _pack_weights(w_in, w_branch, w_out, ffn_up, ffn_down)

    bp = x_prompt.shape[0]
    dt = x_prompt.dtype
    z_win = jnp.zeros((bp, WINDOW, 2, NSA_G, NSA_HD), dt)
    z_rw = jnp.zeros((bp, RW_H, RW_HD, RW_HD), dt)
    z_shift = jnp.zeros((bp, RW_COLS), dt)
    z_pool = jnp.zeros((bp, POOL_BUF, BRANCH_W), dt)
    z_hg = jnp.zeros((bp, HG_H, HG_DK, HG_DV), dt)
    z_ffn = jnp.zeros((bp, CONV_W - 1, D_FF), dt)
    xp = (x_prompt, x_prompt.reshape(-1, D_MODEL).astype(jnp.bfloat16))
    new_p = []
    for l in range(DEPTH):
        xp, st = _layer(l, xp, 0, None, z_win, z_rw, z_shift, z_pool, z_hg, z_ffn, wts, prm, lb_all, win_keep)
        new_p.append(st)

    db = x_sample.shape[0]
    xs = (x_sample, x_sample.reshape(-1, D_MODEL).astype(jnp.bfloat16))
    new_s = []
    for l in range(DEPTH):
        kv_past = cache_kv[l][page_table].reshape(db, past_len, 4, NSA_G, NSA_HD)
        xs, st = _layer(l, xs, past_len, kv_past, state_win[l], state_rwkv[l], state_rwkv_shift[l],
                        state_pool[l], state_hgrn[l], state_ffn_conv[l], wts, prm, lb_all, win_keep)
        new_s.append(st)

    sp = [jnp.stack([st[i] for st in new_p]) for i in range(7)]
    ss = [jnp.stack([st[i] for st in new_s]) for i in range(7)]
    return (xp[0], xs[0], sp[0], ss[0], sp[1], ss[1], sp[2], ss[2], sp[3], ss[3], sp[4], ss[4], sp[5], ss[5], sp[6], ss[6])
```

```python
import functools
import math

import jax
import jax.numpy as jnp
import numpy as np
from jax import lax
from jax.experimental import pallas as pl
from jax.experimental.pallas import tpu as pltpu

D_MODEL = 4096
DEPTH = 4
N_BRANCH = 4
BRANCH_W = D_MODEL // N_BRANCH
RW_HD = 64
RW_H = BRANCH_W // RW_HD
RW_W_RANK = 64
RW_A_RANK = 64
RW_G_RANK = 128
RW_COLS = 3 * BRANCH_W + RW_W_RANK + RW_A_RANK + RW_G_RANK
RW_GN_EPS = 64e-5
POOL_WINDOWS = (2, 4, 8, 16)
POOL_GW = BRANCH_W // len(POOL_WINDOWS)
POOL_BUF = max(POOL_WINDOWS) - 1
HG_DK = 128
HG_H = BRANCH_W // HG_DK
HG_DV = BRANCH_W // HG_H
HG_FDIM = HG_H * HG_DK
HG_COLS = 2 * HG_FDIM + 2 * BRANCH_W
HG_CHUNK = 64
HG_EPS = 1e-6
NSA_HD = 64
NSA_H = BRANCH_W // NSA_HD
NSA_G = 4
NSA_R = NSA_H // NSA_G
NSA_SCALE = NSA_HD ** -0.5
CMP_BLOCK = 32
CMP_STRIDE = 16
CMP_HIDDEN = 128
SEL_BLOCK = 64
N_SEL = 8
WINDOW = 512
Q_BLOCK = 128
NSA_KV_COLS = 6 * NSA_G * NSA_HD
NSA_COLS = BRANCH_W + NSA_KV_COLS + 3 * NSA_H
GATE_COLS = N_BRANCH * D_MODEL
IN_COLS = RW_COLS + BRANCH_W + HG_COLS + NSA_COLS + GATE_COLS
D_FF = 256 * ((8 * D_MODEL // 3 + 255) // 256)
CONV_W = 3
ALPHA = (2 * DEPTH) ** 0.25
LN_EPS = 1e-5
NEG = -1e30

RW_OFF = 0
POOL_OFF = RW_OFF + RW_COLS
HG_OFF = POOL_OFF + BRANCH_W
NSA_OFF = HG_OFF + HG_COLS
MIX_COLS = NSA_OFF + NSA_COLS
IN_TN = 256
NSA_COLS_PAD = -(-NSA_COLS // IN_TN) * IN_TN

VMEM_LIMIT = 56 * 1024 * 1024
LANES = 128
SUBLANES = 8


def _mm_kernel(a_ref, b_ref, o_ref):
    b = b_ref[...].astype(a_ref.dtype)
    o_ref[...] = jnp.dot(a_ref[...], b, preferred_element_type=jnp.float32).astype(o_ref.dtype)


def _matmul(a, w, l, tm, tn, col0=0, n=None, out_dtype=jnp.float32):
    m, k = a.shape
    n = w.shape[2] - col0 if n is None else n
    tm = min(tm, m)
    assert m % tm == 0 and n % tn == 0 and col0 % tn == 0 and col0 + n <= w.shape[2], (m, n, tm, tn, col0)
    c0 = col0 // tn
    return pl.pallas_call(
        _mm_kernel,
        grid=(m // tm, n // tn),
        in_specs=[pl.BlockSpec((tm, k), lambda i, j: (i, 0)),
                  pl.BlockSpec((None, k, tn), lambda i, j: (l, 0, c0 + j))],
        out_specs=pl.BlockSpec((tm, tn), lambda i, j: (i, j)),
        out_shape=jax.ShapeDtypeStruct((m, n), out_dtype),
        compiler_params=pltpu.CompilerParams(
            dimension_semantics=("arbitrary", "arbitrary"), vmem_limit_bytes=VMEM_LIMIT),
        name="matmul",
    )(a, w)


def _mm_ksplit_kernel(a_ref, b_ref, o_ref):
    @pl.when(pl.program_id(2) == 0)
    def _():
        o_ref[...] = jnp.zeros_like(o_ref)

    o_ref[...] += jnp.dot(a_ref[...], b_ref[...], preferred_element_type=jnp.float32)


def _matmul_ksplit(a, w, l, tm, tn, tk):
    m, k = a.shape
    n = w.shape[2]
    tm = min(tm, m)
    assert m % tm == 0 and n % tn == 0 and k % tk == 0
    return pl.pallas_call(
        _mm_ksplit_kernel,
        grid=(m // tm, n // tn, k // tk),
        in_specs=[pl.BlockSpec((tm, tk), lambda i, j, kk: (i, kk)),
                  pl.BlockSpec((None, tk, tn), lambda i, j, kk: (l, kk, j))],
        out_specs=pl.BlockSpec((tm, tn), lambda i, j, kk: (i, j)),
        out_shape=jax.ShapeDtypeStruct((m, n), jnp.float32),
        compiler_params=pltpu.CompilerParams(
            dimension_semantics=("arbitrary", "arbitrary", "arbitrary"), vmem_limit_bytes=VMEM_LIMIT),
        name="matmul_ksplit",
    )(a, w)


def _merge_kernel(x_ref, wg_ref, h_ref, wb_ref, o_ref):
    n = pl.program_id(2)

    @pl.when(n == 0)
    def _():
        o_ref[...] = jnp.zeros_like(o_ref)

    gate = jax.nn.sigmoid(jnp.dot(x_ref[...], wg_ref[...], preferred_element_type=jnp.float32))
    y = jnp.dot(h_ref[0], wb_ref[0].astype(h_ref.dtype), preferred_element_type=jnp.float32)
    o_ref[...] += gate * y


def _gated_merge(x, w_gate, h, w_branch, l, tm, tn):
    nb, m, bw = h.shape
    d = w_branch.shape[-1]
    tm = min(tm, m)
    nj = d // tn
    assert m % tm == 0 and d % tn == 0
    return pl.pallas_call(
        _merge_kernel,
        grid=(m // tm, nj, nb),
        in_specs=[pl.BlockSpec((tm, d), lambda i, j, n: (i, 0)),
                  pl.BlockSpec((None, d, tn), lambda i, j, n: (l, 0, n * nj + j)),
                  pl.BlockSpec((1, tm, bw), lambda i, j, n: (n, i, 0)),
                  pl.BlockSpec((None, 1, bw, tn), lambda i, j, n: (l, n, 0, j))],
        out_specs=pl.BlockSpec((tm, tn), lambda i, j, n: (i, j)),
        out_shape=jax.ShapeDtypeStruct((m, d), jnp.float32),
        compiler_params=pltpu.CompilerParams(
            dimension_semantics=("arbitrary", "arbitrary", "arbitrary"), vmem_limit_bytes=VMEM_LIMIT),
        name="gated_merge",
    )(x, w_gate, h, w_branch)


LN_ROWS = 256


def _ln_kernel(x_ref, y_ref, w_ref, b_ref, o_ref, ob_ref):
    z = ALPHA * x_ref[...] + y_ref[...]
    mu = jnp.mean(z, axis=-1, keepdims=True)
    zc = z - mu
    var = jnp.mean(zc * zc, axis=-1, keepdims=True)
    o = zc * lax.rsqrt(var + LN_EPS) * w_ref[...] + b_ref[...]
    o_ref[...] = o
    ob_ref[...] = o.astype(ob_ref.dtype)


def _residual_layer_norm(x, y, w, b):
    m, d = x.shape
    tr = min(LN_ROWS, m)
    assert m % tr == 0
    row = pl.BlockSpec((tr, d), lambda i: (i, 0))
    vec = pl.BlockSpec((1, d), lambda i: (0, 0))
    return pl.pallas_call(
        _ln_kernel,
        grid=(m // tr,),
        in_specs=[row, row, vec, vec],
        out_specs=[row, row],
        out_shape=[jax.ShapeDtypeStruct((m, d), jnp.float32), jax.ShapeDtypeStruct((m, d), jnp.bfloat16)],
        compiler_params=pltpu.CompilerParams(dimension_semantics=("arbitrary",), vmem_limit_bytes=VMEM_LIMIT),
        name="residual_layer_norm",
    )(x, y, w.reshape(1, d), b.reshape(1, d))


FFN_ACT_COLS = D_FF // 2


def _ffn_act_kernel(u_ref, halo_ref, buf_ref, v_ref, wc_ref, o_ref, *, blocks_per_seq):
    i = pl.program_id(1)
    u = u_ref[...]
    first = (i % blocks_per_seq) == 0
    halo = halo_ref[...]
    prev = jnp.where(first, buf_ref[0], halo[SUBLANES - (CONV_W - 1):, :])
    ridx = lax.broadcasted_iota(jnp.int32, u.shape, 0)
    c = wc_ref[CONV_W - 1:CONV_W, :] * u
    for back in range(1, CONV_W):
        sh = pltpu.roll(u, back, axis=0)
        for j in range(back):
            sh = jnp.where(ridx == j, prev[CONV_W - 1 - back + j:CONV_W - back + j, :], sh)
        c = c + wc_ref[CONV_W - 1 - back:CONV_W - back, :] * sh
    o_ref[...] = (jax.nn.gelu(c) * v_ref[...]).astype(o_ref.dtype)


def _ffn_activation(up, buf, w_conv, seq_len, tr):
    m = up.shape[0]
    tr = min(tr, seq_len)
    tc = FFN_ACT_COLS
    ncb = D_FF // tc
    assert seq_len % tr == 0 and tr % SUBLANES == 0 and m % seq_len == 0
    bps = seq_len // tr
    hb = tr // SUBLANES
    return pl.pallas_call(
        functools.partial(_ffn_act_kernel, blocks_per_seq=bps),
        grid=(ncb, m // tr),
        in_specs=[pl.BlockSpec((tr, tc), lambda j, i: (i, j)),
                  pl.BlockSpec((SUBLANES, tc), lambda j, i: (jnp.maximum(i * hb - 1, 0), j)),
                  pl.BlockSpec((1, CONV_W - 1, tc), lambda j, i: (i // bps, 0, j)),
                  pl.BlockSpec((tr, tc), lambda j, i: (i, ncb + j)),
                  pl.BlockSpec((CONV_W, tc), lambda j, i: (0, j))],
        out_specs=pl.BlockSpec((tr, tc), lambda j, i: (i, j)),
        out_shape=jax.ShapeDtypeStruct((m, D_FF), jnp.bfloat16),
        compiler_params=pltpu.CompilerParams(
            dimension_semantics=("arbitrary", "arbitrary"), vmem_limit_bytes=VMEM_LIMIT),
        name="ffn_activation",
    )(up, up, buf, up, w_conv)


RW_TB = 64
RW_NK = 5


def _rw_rec_kernel(kx_ref, v_ref, s0_ref, y_ref, st_ref, s_scr, *, tb, vs):
    i = pl.program_id(0)

    @pl.when(i == 0)
    def _():
        s_scr[...] = s0_ref[...]

    def step(t, carry):
        def group(gi, c2):
            base = pl.multiple_of(gi * SUBLANES, SUBLANES)
            vrows = v_ref[t, pl.ds(base, SUBLANES), :]
            ys = []
            for j in range(SUBLANES):
                s = s_scr[base + j]
                sa = jnp.sum(s * kx_ref[t, 0], axis=0, keepdims=True)
                s = s * kx_ref[t, 2] + sa * kx_ref[t, 1] + vrows[j:j + 1, :] * kx_ref[t, 3]
                s_scr[base + j] = s
                ys.append(jnp.sum(s * kx_ref[t, 4], axis=0, keepdims=True))
            y_ref[t, pl.ds(base, SUBLANES), :] = jnp.concatenate(ys, axis=0)
            return c2

        lax.fori_loop(0, vs // SUBLANES, group, 0)
        return carry

    lax.fori_loop(0, tb, step, 0)

    @pl.when(i == pl.num_programs(0) - 1)
    def _():
        st_ref[...] = s_scr[...]


def _rwkv_recurrence(rh, k2h, vh, kkh, ah, dh, s0):
    b, t, h, hd = rh.shape
    fold = LANES // (b * h)
    vs = hd // fold
    tb = min(RW_TB, t)
    assert fold * b * h == LANES and t % tb == 0 and vs % SUBLANES == 0
    kx = jnp.stack([-kkh, kkh * ah, dh, k2h, rh])
    kx = kx.transpose(2, 0, 4, 1, 3).reshape(t, RW_NK, hd, b * h)
    kx = jnp.tile(kx, (1, 1, 1, fold))
    vv = vh.reshape(b, t, h, fold, vs).transpose(1, 4, 3, 0, 2).reshape(t, vs, LANES)
    s0l = s0.reshape(b, h, fold, vs, hd).transpose(3, 4, 2, 0, 1).reshape(vs, hd, LANES)
    y, st = pl.pallas_call(
        functools.partial(_rw_rec_kernel, tb=tb, vs=vs),
        grid=(t // tb,),
        in_specs=[pl.BlockSpec((tb, RW_NK, hd, LANES), lambda i: (i, 0, 0, 0)),
                  pl.BlockSpec((tb, vs, LANES), lambda i: (i, 0, 0)),
                  pl.BlockSpec((vs, hd, LANES), lambda i: (0, 0, 0))],
        out_specs=[pl.BlockSpec((tb, vs, LANES), lambda i: (i, 0, 0)),
                   pl.BlockSpec((vs, hd, LANES), lambda i: (0, 0, 0))],
        out_shape=[jax.ShapeDtypeStruct((t, vs, LANES), jnp.float32),
                   jax.ShapeDtypeStruct((vs, hd, LANES), jnp.float32)],
        scratch_shapes=[pltpu.VMEM((vs, hd, LANES), jnp.float32)],
        compiler_params=pltpu.CompilerParams(
            dimension_semantics=("arbitrary",), vmem_limit_bytes=VMEM_LIMIT),
        name="rwkv_recurrence",
    )(kx, vv, s0l)
    y = y.reshape(t, vs, fold, b, h).transpose(3, 0, 4, 2, 1).reshape(b, t, h, hd)
    st = st.reshape(vs, hd, fold, b, h).transpose(3, 4, 2, 0, 1).reshape(b, h, hd, hd)
    return y, st


HG_CHUNK_ROWS = 128
HG_SUB = 16
HG_BLOCK_ROWS = 256


def _hgrn_kernel(q_ref, f_ref, i_ref, g_ref, lb_ref, nw_ref, s0_ref, o_ref, st_ref, s_scr, *, n_valid, cdt):
    f32 = jnp.float32
    ch, sub = HG_CHUNK_ROWS, HG_SUB
    tb = pl.program_id(2)

    @pl.when(tb == 0)
    def _():
        s_scr[...] = s0_ref[0, 0]

    lb = lb_ref[...]
    log_lb = jnp.log(lb)
    log_1mlb = jnp.log1p(-lb)
    rows = q_ref.shape[0]
    tril = (lax.broadcasted_iota(jnp.int32, (ch, ch), 0) >= lax.broadcasted_iota(jnp.int32, (ch, ch), 1)).astype(f32)
    rowc = lax.broadcasted_iota(jnp.int32, (ch, HG_DK), 0)
    trow = lax.broadcasted_iota(jnp.int32, (sub, HG_DK), 0)
    lane = lax.broadcasted_iota(jnp.int32, (sub, ch), 1)
    nt = (((1,), (1,)), ((), ()))
    for c in range(rows // ch):
        sl = slice(c * ch, (c + 1) * ch)
        ff = f_ref[sl, :]
        log_sig = jnp.minimum(ff, 0.0) - jnp.log1p(jnp.exp(-jnp.abs(ff)))
        x2 = log_1mlb + log_sig
        log_f = jnp.maximum(log_lb, x2) + jnp.log1p(jnp.exp(-jnp.abs(log_lb - x2)))
        kk = (1.0 - lb) * jax.nn.sigmoid(-ff)
        if n_valid < rows:
            ok = rowc < (n_valid - c * ch)
            log_f = jnp.where(ok, log_f, 0.0)
            kk = jnp.where(ok, kk, 0.0)
        q = q_ref[sl, :]
        qq = q * jax.nn.sigmoid(q)
        vv = i_ref[sl, :]
        b = jnp.dot(tril, log_f, preferred_element_type=f32, precision=lax.Precision.HIGHEST)
        st = s_scr[...]
        inter = lax.dot_general((qq * jnp.exp(b)).astype(cdt), st.astype(cdt), nt, preferred_element_type=f32)
        strips = []
        for i in range(ch // sub):
            r0 = i * sub
            bi = b[r0:r0 + sub]
            qi = qq[r0:r0 + sub]
            ki = kk[r0:r0 + sub]
            if i > 0:
                bprev = b[r0 - 1:r0]
                ks = jnp.where(rowc < r0, kk * jnp.exp(jnp.minimum(bprev - b, 0.0)), 0.0)
                qs = qi * jnp.exp(bi - bprev)
                strip = lax.dot_general(qs.astype(cdt), ks.astype(cdt), nt, preferred_element_type=f32)
            else:
                strip = jnp.zeros((sub, ch), f32)
            for s in range(sub):
                keep = trow >= s
                e = jnp.exp(jnp.where(keep, bi - bi[s:s + 1, :], 0.0))
                col = jnp.sum(jnp.where(keep, qi * ki[s:s + 1, :] * e, 0.0), axis=-1, keepdims=True)
                strip = jnp.where(lane == r0 + s, col, strip)
            strips.append(strip)
        att = jnp.concatenate(strips, axis=0)
        o = inter + jnp.dot(att.astype(cdt), vv.astype(cdt), preferred_element_type=f32)
        bl = b[ch - 1:ch, :]
        kd = kk * jnp.exp(bl - b)
        s_scr[...] = st * jnp.exp(bl) + jnp.dot(vv.T.astype(cdt), kd.astype(cdt), preferred_element_type=f32)
        o = o * lax.rsqrt(jnp.mean(o * o, axis=-1, keepdims=True) + HG_EPS)
        g = g_ref[sl, :]
        o_ref[sl, :] = (o * nw_ref[...] * (g * jax.nn.sigmoid(g))).astype(o_ref.dtype)

    @pl.when(tb == pl.num_programs(2) - 1)
    def _():
        st_ref[0, 0] = s_scr[...]


def _hgrn2_pallas(p, s0, lb, norm_w, cdt=jnp.bfloat16, out_dtype=jnp.bfloat16):
    b, t, _ = p.shape
    assert HG_DK == LANES and HG_DV == LANES
    rows = min(HG_BLOCK_ROWS, -(-t // HG_CHUNK_ROWS) * HG_CHUNK_ROWS)
    tp = -(-t // rows) * rows
    n_valid = rows if tp == t else t
    assert tp == t or tp == rows
    if tp != t:
        p = jnp.pad(p, ((0, 0), (0, tp - t), (0, 0)))
    p2 = p.reshape(b * tp, HG_COLS)
    nt = tp // rows
    s0t = jnp.swapaxes(s0, 2, 3)
    col = lambda off: pl.BlockSpec((rows, LANES), lambda bi, hi, ti: (bi * nt + ti, off + hi))
    vec = pl.BlockSpec((1, LANES), lambda bi, hi, ti: (0, hi))
    sspec = pl.BlockSpec((1, 1, HG_DV, HG_DK), lambda bi, hi, ti: (bi, hi, 0, 0))
    o, st = pl.pallas_call(
        functools.partial(_hgrn_kernel, n_valid=n_valid, cdt=cdt),
        grid=(b, HG_H, nt),
        in_specs=[col(0), col(HG_H), col(2 * HG_H), col(3 * HG_H), vec, vec, sspec],
        out_specs=[pl.BlockSpec((rows, LANES), lambda bi, hi, ti: (bi * nt + ti, hi)), sspec],
        out_shape=[jax.ShapeDtypeStruct((b * tp, BRANCH_W), out_dtype),
                   jax.ShapeDtypeStruct((b, HG_H, HG_DV, HG_DK), jnp.float32)],
        scratch_shapes=[pltpu.VMEM((HG_DV, HG_DK), jnp.float32)],
        compiler_params=pltpu.CompilerParams(
            dimension_semantics=("arbitrary", "arbitrary", "arbitrary"), vmem_limit_bytes=VMEM_LIMIT),
        name="hgrn2",
    )(p2, p2, p2, p2, lb.reshape(1, HG_FDIM), norm_w.reshape(1, BRANCH_W), s0t)
    if tp != t:
        o = o.reshape(b, tp, BRANCH_W)[:, :t].reshape(b * t, BRANCH_W)
    return o, jnp.swapaxes(st, 2, 3)


def _nsa_kernel(q_ref, kc_ref, vc_ref, ks_ref, vs_ref, kw_ref, vw_ref, gl_ref, ov_ref, e_ref, o_ref,
                *, n_keys, wb):
    f32 = jnp.float32
    cdt = q_ref.dtype
    qb = Q_BLOCK
    qi = pl.program_id(2)
    nt = (((1,), (1,)), ((), ()))
    tq = lax.broadcasted_iota(jnp.int32, (qb, 1), 0) + qi * qb

    def softmax_parts(s, mask):
        s = jnp.where(mask, s, NEG)
        e = jnp.exp(s - jnp.max(s, axis=-1, keepdims=True))
        return e, jnp.sum(e, axis=-1, keepdims=True)

    cidx = lax.broadcasted_iota(jnp.int32, (qb, LANES), 1)
    maskc = (cidx * CMP_STRIDE + (CMP_BLOCK - 1)) <= tq
    kc = kc_ref[0, 0]
    vc = vc_ref[0, 0]
    psum = jnp.zeros((qb, LANES), f32)
    o_cmp = []
    for r in range(NSA_R):
        q_r = q_ref[0, 0, 0, r * qb:(r + 1) * qb, :]
        s = lax.dot_general(q_r, kc, nt, preferred_element_type=f32) * NSA_SCALE
        e, l = softmax_parts(s, maskc)
        pr = jnp.where(maskc, e / l, 0.0)
        psum = psum + pr
        o_cmp.append(jnp.dot(pr.astype(cdt), vc, preferred_element_type=f32))

    imp = jnp.dot(psum, ov_ref[...], preferred_element_type=f32, precision=lax.Precision.HIGHEST)
    cur = lax.shift_right_logical(tq, int(math.log2(SEL_BLOCK)))
    sc = jnp.where(cidx <= cur, imp, -jnp.inf)
    sc = jnp.where(cidx == cur, jnp.inf, sc)
    ns = n_keys // SEL_BLOCK
    sct = sc.T[:ns]
    sidx = lax.broadcasted_iota(jnp.int32, (ns, qb), 0)
    rank = jnp.zeros((ns, qb), jnp.int32)
    for sp in range(ns):
        row = sct[sp:sp + 1, :]
        beats = (row > sct) | ((row == sct) & (sidx > sp))
        rank = rank + beats.astype(jnp.int32)
    selt = (rank < N_SEL).astype(f32)
    if ns < LANES:
        selt = jnp.concatenate([selt, jnp.zeros((LANES - ns, qb), f32)], axis=0)
    msel = jnp.dot(selt.T.astype(jnp.bfloat16), e_ref[...], preferred_element_type=f32)
    kpos = lax.broadcasted_iota(jnp.int32, (qb, n_keys), 1)
    valid = (msel > 0.5) & (kpos <= tq)

    kposw = lax.broadcasted_iota(jnp.int32, (qb, wb + qb), 1) + (qi * qb - wb)
    rel = tq - kposw
    maskw = (rel >= 0) & (rel < WINDOW) & (kposw >= 0)
    wstart = pl.multiple_of(qi * qb, qb)
    kw = kw_ref[0, 0, pl.ds(wstart, wb + qb), :]
    vw = vw_ref[0, 0, pl.ds(wstart, wb + qb), :]

    ks = ks_ref[0, 0]
    vs = vs_ref[0, 0]
    outs = []
    for r in range(NSA_R):
        q_r = q_ref[0, 0, 0, r * qb:(r + 1) * qb, :]
        s = lax.dot_general(q_r, ks, nt, preferred_element_type=f32) * NSA_SCALE
        e, l = softmax_parts(s, valid)
        o_sel = jnp.dot(e.astype(cdt), vs, preferred_element_type=f32) / l
        s = lax.dot_general(q_r, kw, nt, preferred_element_type=f32) * NSA_SCALE
        e, l = softmax_parts(s, maskw)
        o_win = jnp.dot(e.astype(cdt), vw, preferred_element_type=f32) / l
        g = jax.nn.sigmoid(gl_ref[0, 0, 0, r * qb:(r + 1) * qb, :])
        outs.append(g[:, 0:1] * o_cmp[r] + g[:, 1:2] * o_sel + g[:, 2:3] * o_win)
    o_ref[...] = jnp.concatenate(outs, axis=1).astype(o_ref.dtype)


def _nsa_prompt(q, kc, vc, kv6, gl, win_buf, cdt, out_dtype):
    b, t, _ = q.shape
    qb = Q_BLOCK
    nqb = t // qb
    wb = win_buf.shape[1]
    nc = kc.shape[1]
    assert t % qb == 0 and nc <= LANES and t // SEL_BLOCK <= LANES and t // SEL_BLOCK >= N_SEL
    qa = q.reshape(b, nqb, qb, NSA_G, NSA_R, NSA_HD).transpose(0, 3, 1, 4, 2, 5)
    qa = qa.reshape(b, NSA_G, nqb, NSA_R * qb, NSA_HD).astype(cdt)
    gla = gl.reshape(b, nqb, qb, 3, NSA_G, NSA_R).transpose(0, 4, 1, 5, 2, 3).reshape(b, NSA_G, nqb, NSA_R * qb, 3)

    def bg(x):
        return x.transpose(0, 2, 1, 3).astype(cdt)

    pad = ((0, 0), (0, LANES - nc), (0, 0), (0, 0))
    kca, vca = bg(jnp.pad(kc, pad)), bg(jnp.pad(vc, pad))
    ksa, vsa = bg(kv6[:, :, 2]), bg(kv6[:, :, 3])
    kwa = bg(jnp.concatenate([win_buf[:, :, 0], kv6[:, :, 4]], axis=1))
    vwa = bg(jnp.concatenate([win_buf[:, :, 1], kv6[:, :, 5]], axis=1))
    ns = t // SEL_BLOCK
    ov = np.zeros((LANES, LANES), np.float32)
    cs = np.arange(nc)[:, None] * CMP_STRIDE
    ss = np.arange(ns)[None, :] * SEL_BLOCK
    ov[:nc, :ns] = (cs < ss + SEL_BLOCK) & (cs + CMP_BLOCK > ss)
    e = (np.arange(LANES)[:, None] == (np.arange(t)[None, :] // SEL_BLOCK)).astype(np.float32)
    kvspec = lambda n: pl.BlockSpec((1, 1, n, NSA_HD), lambda bi, gi, qi: (bi, gi, 0, 0))
    return pl.pallas_call(
        functools.partial(_nsa_kernel, n_keys=t, wb=wb),
        grid=(b, NSA_G, nqb),
        in_specs=[pl.BlockSpec((1, 1, 1, NSA_R * qb, NSA_HD), lambda bi, gi, qi: (bi, gi, qi, 0, 0)),
                  kvspec(LANES), kvspec(LANES), kvspec(t), kvspec(t), kvspec(wb + t), kvspec(wb + t),
                  pl.BlockSpec((1, 1, 1, NSA_R * qb, 3), lambda bi, gi, qi: (bi, gi, qi, 0, 0)),
                  pl.BlockSpec((LANES, LANES), lambda bi, gi, qi: (0, 0)),
                  pl.BlockSpec((LANES, t), lambda bi, gi, qi: (0, 0))],
        out_specs=pl.BlockSpec((qb, NSA_R * NSA_HD), lambda bi, gi, qi: (bi * nqb + qi, gi)),
        out_shape=jax.ShapeDtypeStruct((b * t, BRANCH_W), out_dtype),
        compiler_params=pltpu.CompilerParams(
            dimension_semantics=("arbitrary", "arbitrary", "arbitrary"), vmem_limit_bytes=VMEM_LIMIT),
        name="nsa_prompt",
    )(qa, kca, vca, ksa, vsa, kwa, vwa, gla, jnp.asarray(ov), jnp.asarray(e, jnp.bfloat16))


def _split(x, sizes):
    idx = tuple(int(s) for s in np.cumsum(sizes)[:-1])
    return jnp.split(x, idx, axis=-1)


def _rwkv7(p, shift_prev, S0, mu, w0, w2, a0, a2, g2, k_k, k_a, r_k, ln_w, ln_b):
    B, T, _ = p.shape
    f32 = jnp.float32
    p_prev = jnp.concatenate([shift_prev[:, None], p[:, :-1]], axis=1)
    xs = p + mu * (p_prev - p)
    r, k, v, wl, al, gl = _split(xs, (BRANCH_W, BRANCH_W, BRANCH_W, RW_W_RANK, RW_A_RANK, RW_G_RANK))
    w = -jax.nn.softplus(-(w0 + jnp.tanh(wl) @ w2).astype(f32)) - 0.5
    decay = jnp.exp(-jnp.exp(w))
    a = jax.nn.sigmoid((a0 + al @ a2).astype(f32))
    g = jax.nn.sigmoid(gl) @ g2

    def heads(t):
        return t.astype(f32).reshape(B, T, RW_H, RW_HD)

    kk = heads(k * k_k)
    kk = kk / jnp.maximum(jnp.linalg.norm(kk, axis=-1, keepdims=True), 1e-12)
    k2 = k.astype(f32) * (1.0 + (a - 1.0) * k_a)
    rh, kh, vh, ah, dh = heads(r), heads(k2), heads(v), heads(a), heads(decay)

    y, S_T = _rwkv_recurrence(rh, kh, vh, kk, ah, dh, S0.astype(f32))
    mean = y.mean(-1, keepdims=True)
    var = jnp.mean(jnp.square(y - mean), -1, keepdims=True)
    yn = ((y - mean) * lax.rsqrt(var + RW_GN_EPS)).reshape(B, T, BRANCH_W) * ln_w + ln_b
    bonus = (jnp.sum(rh * kh * r_k, -1, keepdims=True) * vh).reshape(B, T, BRANCH_W)
    out = (yn + bonus) * g
    return out, S_T, p[:, -1]


def _pool(p, buf, pos0, w_grp, scale):
    B, T, _ = p.shape
    xc = jnp.concatenate([buf, p], axis=1)
    cs = jnp.cumsum(xc.astype(jnp.float32), axis=1)
    cs = jnp.pad(cs, ((0, 0), (1, 0), (0, 0)))
    end = cs[:, POOL_BUF + 1: POOL_BUF + 1 + T]
    t = jnp.arange(T)
    outs = []
    for gi, w in enumerate(POOL_WINDOWS):
        lo, hi = gi * POOL_GW, (gi + 1) * POOL_GW
        start = cs[:, POOL_BUF + 1 - w: POOL_BUF + 1 - w + T, lo:hi]
        cnt = jnp.minimum(w, pos0 + t + 1).astype(jnp.float32)[None, :, None]
        outs.append((end[..., lo:hi] - start) / cnt - p[..., lo:hi].astype(jnp.float32))
    d = jnp.stack(outs, axis=2)
    y = jnp.einsum('btgc,gcd->btgd', d.astype(p.dtype), w_grp).reshape(B, T, BRANCH_W) * scale
    return y, xc[:, -POOL_BUF:]


def _compress(kv, pe, w1, w2):
    B, N = kv.shape[:2]
    nc = (N - CMP_BLOCK) // CMP_STRIDE + 1
    idx = jnp.arange(nc)[:, None] * CMP_STRIDE + jnp.arange(CMP_BLOCK)[None, :]
    blocks = kv[:, idx] + pe[None, None, :, None, :]
    blocks = blocks.transpose(0, 1, 3, 2, 4).reshape(B, nc, NSA_G, CMP_BLOCK * NSA_HD)
    return jax.nn.gelu(blocks @ w1) @ w2


def _overlap(nc, ns):
    cs = jnp.arange(nc)[:, None] * CMP_STRIDE
    ss = jnp.arange(ns)[None, :] * SEL_BLOCK
    return ((cs < ss + SEL_BLOCK) & (cs + CMP_BLOCK > ss)).astype(jnp.float32)


def _cmp_attention(qg, kc, vc, pos0):
    T = qg.shape[1]
    nc = kc.shape[1]
    s = jnp.einsum('btgrd,bcgd->bgrtc', qg, kc).astype(jnp.float32) * NSA_SCALE
    qpos = pos0 + jnp.arange(T)
    bend = jnp.arange(nc) * CMP_STRIDE + CMP_BLOCK - 1
    mask = bend[None, :] <= qpos[:, None]
    pr = jax.nn.softmax(jnp.where(mask, s, NEG), axis=-1) * mask
    o = jnp.einsum('bgrtc,bcgd->btgrd', pr.astype(vc.dtype), vc)
    return o, pr


def _sel_attention(qg, k, v, imp, pos0):
    B, T = qg.shape[:2]
    N = k.shape[1]
    ns = imp.shape[-1]
    n_sel = min(N_SEL, ns)
    padn = ns * SEL_BLOCK - N
    kb = jnp.pad(k, ((0, 0), (0, padn), (0, 0), (0, 0))).reshape(B, ns, SEL_BLOCK, NSA_G, NSA_HD).transpose(0, 3, 1, 2, 4)
    vb = jnp.pad(v, ((0, 0), (0, padn), (0, 0), (0, 0))).reshape(B, ns, SEL_BLOCK, NSA_G, NSA_HD).transpose(0, 3, 1, 2, 4)
    qpos = pos0 + jnp.arange(T)
    cur = qpos // SEL_BLOCK
    blk = jnp.arange(ns)
    sc = jnp.where(blk[None, :] <= cur[:, None], imp, -jnp.inf)
    sc = jnp.where(blk[None, :] == cur[:, None], jnp.inf, sc)
    _, idx = lax.top_k(sc, n_sel)
    Qb = min(Q_BLOCK, T)
    nqb = -(-T // Qb)
    Tp = nqb * Qb
    q_all = jnp.pad(qg, ((0, 0), (0, Tp - T), (0, 0), (0, 0), (0, 0))).reshape(B, nqb, Qb, NSA_G, NSA_R, NSA_HD).transpose(1, 0, 2, 3, 4, 5)
    idx_all = jnp.pad(idx, ((0, 0), (0, 0), (0, Tp - T), (0, 0))).reshape(B, NSA_G, nqb, Qb, n_sel).transpose(2, 0, 1, 3, 4)
    pos_all = (pos0 + jnp.arange(Tp)).reshape(nqb, Qb)
    bi = jnp.arange(B)[:, None, None, None]
    gi = jnp.arange(NSA_G)[None, :, None, None]
    lpos = jnp.arange(SEL_BLOCK)

    def block(inp):
        qb, ib, pb = inp
        kg = kb[bi, gi, ib]
        vg = vb[bi, gi, ib]
        s = jnp.einsum('bqgrd,bgqnld->bgrqnl', qb, kg).astype(jnp.float32) * NSA_SCALE
        kpos = ib[..., None] * SEL_BLOCK + lpos
        valid = kpos <= pb[None, None, :, None, None]
        s = jnp.where(valid[:, :, None], s, NEG)
        pr = jax.nn.softmax(s.reshape(B, NSA_G, NSA_R, Qb, n_sel * SEL_BLOCK), axis=-1).reshape(s.shape)
        return jnp.einsum('bgrqnl,bgqnld->bqgrd', pr.astype(vg.dtype), vg)

    o = lax.map(block, (q_all, idx_all, pos_all))
    return jnp.moveaxis(o, 0, 1).reshape(B, Tp, NSA_G, NSA_R, NSA_HD)[:, :T]


def _win_attention(qg, kv_new, kv_buf, pos0):
    B, T = qg.shape[:2]
    wb = kv_buf.shape[1]
    kv_cat = jnp.concatenate([kv_buf, kv_new], axis=1)
    Qb = min(Q_BLOCK, T)
    nqb = -(-T // Qb)
    Tp = nqb * Qb
    kv_pad = jnp.pad(kv_cat, ((0, 0), (0, Tp - T), (0, 0), (0, 0), (0, 0)))
    q_pad = jnp.pad(qg, ((0, 0), (0, Tp - T), (0, 0), (0, 0), (0, 0)))

    def block(c):
        start = c * Qb
        qs = lax.dynamic_slice_in_dim(q_pad, start, Qb, axis=1)
        kvs = lax.dynamic_slice_in_dim(kv_pad, start, wb + Qb, axis=1)
        qpos = pos0 + start + jnp.arange(Qb)
        kpos = pos0 - wb + start + jnp.arange(wb + Qb)
        rel = qpos[:, None] - kpos[None, :]
        mask = (rel >= 0) & (rel < WINDOW) & (kpos[None, :] >= 0)
        s = jnp.einsum('bqgrd,bkgd->bgrqk', qs, kvs[:, :, 0]).astype(jnp.float32) * NSA_SCALE
        pr = jax.nn.softmax(jnp.where(mask, s, NEG), axis=-1)
        return jnp.einsum('bgrqk,bkgd->bqgrd', pr.astype(kvs.dtype), kvs[:, :, 1])

    o = lax.map(block, jnp.arange(nqb))
    o = jnp.moveaxis(o, 0, 1).reshape(B, Tp, NSA_G, NSA_R, NSA_HD)[:, :T]
    return o, kv_cat


def _nsa(p, kv_past, win_buf, pos0, pe, w1, w2):
    B, T, _ = p.shape
    q, kv6, gl = _split(p, (BRANCH_W, NSA_KV_COLS, 3 * NSA_H))
    qg = q.reshape(B, T, NSA_G, NSA_R, NSA_HD)
    kv6 = kv6.reshape(B, T, 6, NSA_G, NSA_HD)
    kv_new = kv6[:, :, :4]
    kv_all = kv_new if kv_past is None else jnp.concatenate([kv_past, kv_new], axis=1)
    kc = _compress(kv_all[:, :, 0], pe[0], w1[0], w2[0])
    vc = _compress(kv_all[:, :, 1], pe[1], w1[1], w2[1])
    o_cmp, p_cmp = _cmp_attention(qg, kc, vc, pos0)
    ns = -(-kv_all.shape[1] // SEL_BLOCK)
    imp = jnp.einsum('bgrtc,cs->bgts', p_cmp, _overlap(kc.shape[1], ns))
    o_sel = _sel_attention(qg, kv_all[:, :, 2], kv_all[:, :, 3], imp, pos0)
    o_win, win_cat = _win_attention(qg, kv6[:, :, 4:], win_buf, pos0)
    g = jax.nn.sigmoid(gl.reshape(B, T, 3, NSA_G, NSA_R))[..., None]
    o = g[:, :, 0] * o_cmp + g[:, :, 1] * o_sel + g[:, :, 2] * o_win
    return o.reshape(B, T, BRANCH_W), kv_new, win_cat


def _nsa_nopast(p, win_buf, pe, w1, w2, cdt=jnp.bfloat16, out_dtype=jnp.bfloat16):
    B, T, _ = p.shape
    q, kv6, gl = _split(p, (BRANCH_W, NSA_KV_COLS, 3 * NSA_H))
    kv6 = kv6.reshape(B, T, 6, NSA_G, NSA_HD)
    kv_new = kv6[:, :, :4]
    kc = _compress(kv6[:, :, 0], pe[0], w1[0], w2[0])
    vc = _compress(kv6[:, :, 1], pe[1], w1[1], w2[1])
    o = _nsa_prompt(q, kc, vc, kv6, gl, win_buf, cdt, out_dtype)
    win_cat = jnp.concatenate([win_buf, kv6[:, :, 4:]], axis=1)
    return o, kv_new, win_cat


def _prep_weights(w_in, ffn_down):
    bf = jnp.bfloat16
    return w_in[:, :, MIX_COLS:].astype(bf), ffn_down.astype(bf)


def _layer(l, xpair, pos0, kv_past, win_buf, rw_S, rw_shift, pool_buf, hg_S, ffn_buf, wts, prm, lb_all, win_keep):
    (w_in, w_gate_b, w_branch, w_out, ffn_up, ffn_down_b) = wts
    x, x2b = xpair
    B, T, _ = x.shape
    m = B * T
    bf = jnp.bfloat16
    x2 = x.reshape(m, D_MODEL)

    def in_proj(col0, n):
        return _matmul(x2b, w_in, l, 1024, IN_TN, col0, n)

    p_rw = in_proj(RW_OFF, RW_COLS).reshape(B, T, RW_COLS)
    p_pool = in_proj(POOL_OFF, BRANCH_W).reshape(B, T, BRANCH_W)
    p_hg = in_proj(HG_OFF, HG_COLS).reshape(B, T, HG_COLS)
    p_nsa = in_proj(NSA_OFF, NSA_COLS_PAD)[:, :NSA_COLS].reshape(B, T, NSA_COLS)
    h_rw, rw_S_new, rw_shift_new = _rwkv7(p_rw, rw_shift, rw_S, prm['rw_mu'][l], prm['rw_w0'][l], prm['rw_w2'][l],
                                          prm['rw_a0'][l], prm['rw_a2'][l], prm['rw_g2'][l], prm['rw_k_k'][l],
                                          prm['rw_k_a'][l], prm['rw_r_k'][l], prm['rw_ln_w'][l], prm['rw_ln_b'][l])
    h_pool, pool_new = _pool(p_pool, pool_buf, pos0, prm['pool_w'][l], prm['pool_scale'][l])
    h_hg, hg_S_new = _hgrn2_pallas(p_hg, hg_S.astype(jnp.float32), lb_all[l], prm['hg_norm_w'][l])
    if kv_past is None:
        h_nsa, kv_new, win_cat = _nsa_nopast(p_nsa, win_buf, prm['cmp_pe'][l], prm['cmp_w1'][l], prm['cmp_w2'][l])
    else:
        h_nsa, kv_new, win_cat = _nsa(p_nsa, kv_past, win_buf, pos0, prm['cmp_pe'][l], prm['cmp_w1'][l],
                                      prm['cmp_w2'][l])
    win_new = win_cat[:, -win_keep:]
    h = jnp.stack([t.reshape(m, BRANCH_W).astype(bf) for t in (h_rw, h_pool, h_hg, h_nsa)])
    merged = _gated_merge(x2b, w_gate_b, h, w_branch, l, 1024, 512)
    y = _matmul(merged.astype(bf), w_out, l, 1024, 512)
    x2, x2b = _residual_layer_norm(x2, y, prm['ln_w'][l, 0], prm['ln_b'][l, 0])
    up = _matmul(x2b, ffn_up, l, 1024, 512)
    hcv = _ffn_activation(up, ffn_buf, prm['ffn_conv'][l], T, 128)
    f = _matmul_ksplit(hcv, ffn_down_b, l, 1024, 512, D_FF // 2)
    x2, x2b = _residual_layer_norm(x2, f, prm['ln_w'][l, 1], prm['ln_b'][l, 1])
    assert T >= CONV_W - 1
    ffn_new = up.reshape(B, T, 2 * D_FF)[:, T - (CONV_W - 1):, :D_FF]
    return (x2.reshape(B, T, D_MODEL), x2b), (kv_new, win_new, rw_S_new, rw_shift_new, pool_new, hg_S_new, ffn_new)


def kernel(x_prompt, x_sample, cache_kv, page_table, state_win, state_rwkv, state_rwkv_shift, state_pool, state_hgrn, state_ffn_conv, w_in, rw_mu, rw_w0, rw_w2, rw_a0, rw_a2, rw_g2, rw_k_k, rw_k_a, rw_r_k, rw_ln_w, rw_ln_b, pool_w, pool_scale, hg_lb_raw, hg_norm_w, cmp_pe, cmp_w1, cmp_w2, w_branch, w_out, ln_w, ln_b, ffn_up, ffn_conv, ffn_down):
    prm = dict(rw_mu=rw_mu, rw_w0=rw_w0, rw_w2=rw_w2, rw_a0=rw_a0, rw_a2=rw_a2, rw_g2=rw_g2, rw_k_k=rw_k_k,
               rw_k_a=rw_k_a, rw_r_k=rw_r_k, rw_ln_w=rw_ln_w, rw_ln_b=rw_ln_b, pool_w=pool_w,
               pool_scale=pool_scale, hg_norm_w=hg_norm_w, cmp_pe=cmp_pe, cmp_w1=cmp_w1, cmp_w2=cmp_w2,
               ln_w=ln_w, ln_b=ln_b, ffn_conv=ffn_conv)
    lb_cum = jnp.cumsum(jax.nn.softmax(hg_lb_raw.astype(jnp.float32), axis=0), axis=0)
    lb_all = lb_cum - lb_cum[:1]
    past_len = page_table.shape[1] * cache_kv.shape[2]
    win_keep = state_win.shape[2]
    w_gate_b, ffn_down_b = _prep_weights(w_in, ffn_down)
    wts = (w_in, w_gate_b, w_branch, w_out, ffn_up, ffn_down_b)

    bp = x_prompt.shape[0]
    dt = x_prompt.dtype
    z_win = jnp.zeros((bp, WINDOW, 2, NSA_G, NSA_HD), dt)
    z_rw = jnp.zeros((bp, RW_H, RW_HD, RW_HD), dt)
    z_shift = jnp.zeros((bp, RW_COLS), dt)
    z_pool = jnp.zeros((bp, POOL_BUF, BRANCH_W), dt)
    z_hg = jnp.zeros((bp, HG_H, HG_DK, HG_DV), dt)
    z_ffn = jnp.zeros((bp, CONV_W - 1, D_FF), dt)
    xp = (x_prompt, x_prompt.reshape(-1, D_MODEL).astype(jnp.bfloat16))
    new_p = []
    for l in range(DEPTH):
        xp, st = _layer(l, xp, 0, None, z_win, z_rw, z_shift, z_pool, z_hg, z_ffn, wts, prm, lb_all, win_keep)
        new_p.append(st)

    db = x_sample.shape[0]
    xs = (x_sample, x_sample.reshape(-1, D_MODEL).astype(jnp.bfloat16))
    new_s = []
    for l in range(DEPTH):
        kv_past = cache_kv[l][page_table].reshape(db, past_len, 4, NSA_G, NSA_HD)
        xs, st = _layer(l, xs, past_len, kv_past, state_win[l], state_rwkv[l], state_rwkv_shift[l],
                        state_pool[l], state_hgrn[l], state_ffn_conv[l], wts, prm, lb_all, win_keep)
        new_s.append(st)

    sp = [jnp.stack([st[i] for st in new_p]) for i in range(7)]
    ss = [jnp.stack([st[i] for st in new_s]) for i in range(7)]
    return (xp[0], xs[0], sp[0], ss[0], sp[1], ss[1], sp[2], ss[2], sp[3], ss[3], sp[4], ss[4], sp[5], ss[5], sp[6], ss[6])
```

```python
import functools
import math

import jax
import jax.numpy as jnp
import numpy as np
from jax import lax
from jax.experimental import pallas as pl
from jax.experimental.pallas import tpu as pltpu

D_MODEL = 4096
DEPTH = 4
N_BRANCH = 4
BRANCH_W = D_MODEL // N_BRANCH
RW_HD = 64
RW_H = BRANCH_W // RW_HD
RW_W_RANK = 64
RW_A_RANK = 64
RW_G_RANK = 128
RW_COLS = 3 * BRANCH_W + RW_W_RANK + RW_A_RANK + RW_G_RANK
RW_GN_EPS = 64e-5
POOL_WINDOWS = (2, 4, 8, 16)
POOL_GW = BRANCH_W // len(POOL_WINDOWS)
POOL_BUF = max(POOL_WINDOWS) - 1
HG_DK = 128
HG_H = BRANCH_W // HG_DK
HG_DV = BRANCH_W // HG_H
HG_FDIM = HG_H * HG_DK
HG_COLS = 2 * HG_FDIM + 2 * BRANCH_W
HG_CHUNK = 64
HG_EPS = 1e-6
NSA_HD = 64
NSA_H = BRANCH_W // NSA_HD
NSA_G = 4
NSA_R = NSA_H // NSA_G
NSA_SCALE = NSA_HD ** -0.5
CMP_BLOCK = 32
CMP_STRIDE = 16
CMP_HIDDEN = 128
SEL_BLOCK = 64
N_SEL = 8
WINDOW = 512
Q_BLOCK = 128
NSA_KV_COLS = 6 * NSA_G * NSA_HD
NSA_COLS = BRANCH_W + NSA_KV_COLS + 3 * NSA_H
GATE_COLS = N_BRANCH * D_MODEL
IN_COLS = RW_COLS + BRANCH_W + HG_COLS + NSA_COLS + GATE_COLS
D_FF = 256 * ((8 * D_MODEL // 3 + 255) // 256)
CONV_W = 3
ALPHA = (2 * DEPTH) ** 0.25
LN_EPS = 1e-5
NEG = -1e30

RW_OFF = 0
POOL_OFF = RW_OFF + RW_COLS
HG_OFF = POOL_OFF + BRANCH_W
NSA_OFF = HG_OFF + HG_COLS
MIX_COLS = NSA_OFF + NSA_COLS
IN_TN = 256
NSA_COLS_PAD = -(-NSA_COLS // IN_TN) * IN_TN

VMEM_LIMIT = 56 * 1024 * 1024
LANES = 128
SUBLANES = 8


GATE_BLK0 = MIX_COLS // LANES
GATE_SHIFT = MIX_COLS % LANES
GATE_TN = 512
GATE_ROWS = 1024


def _gate_repack_kernel(*refs):
    o_ref = refs[-1]
    x = jnp.concatenate([r[...] for r in refs[:-1]], axis=1)
    o_ref[...] = x[:, GATE_SHIFT:GATE_SHIFT + GATE_TN].astype(o_ref.dtype)


def _gate_weights(w_in):
    nl, d, _ = w_in.shape
    nin = GATE_TN // LANES + 1
    specs = [pl.BlockSpec((None, GATE_ROWS, LANES),
                          lambda l, i, j, k=k: (l, i, GATE_BLK0 + (GATE_TN // LANES) * j + k)) for k in range(nin)]
    return pl.pallas_call(
        _gate_repack_kernel,
        grid=(nl, d // GATE_ROWS, GATE_COLS // GATE_TN),
        in_specs=specs,
        out_specs=pl.BlockSpec((None, GATE_ROWS, GATE_TN), lambda l, i, j: (l, i, j)),
        out_shape=jax.ShapeDtypeStruct((nl, d, GATE_COLS), jnp.bfloat16),
        compiler_params=pltpu.CompilerParams(
            dimension_semantics=("arbitrary", "arbitrary", "arbitrary"), vmem_limit_bytes=VMEM_LIMIT),
        name="gate_repack",
    )(*([w_in] * nin))


def _mm_kernel(a_ref, b_ref, o_ref):
    b = b_ref[...].astype(a_ref.dtype)
    o_ref[...] = jnp.dot(a_ref[...], b, preferred_element_type=jnp.float32).astype(o_ref.dtype)


def _matmul(a, w, l, tm, tn, col0=0, n=None, out_dtype=jnp.float32):
    m, k = a.shape
    n = w.shape[2] - col0 if n is None else n
    tm = min(tm, m)
    assert m % tm == 0 and n % tn == 0 and col0 % tn == 0 and col0 + n <= w.shape[2], (m, n, tm, tn, col0)
    c0 = col0 // tn
    return pl.pallas_call(
        _mm_kernel,
        grid=(m // tm, n // tn),
        in_specs=[pl.BlockSpec((tm, k), lambda i, j: (i, 0)),
                  pl.BlockSpec((None, k, tn), lambda i, j: (l, 0, c0 + j))],
        out_specs=pl.BlockSpec((tm, tn), lambda i, j: (i, j)),
        out_shape=jax.ShapeDtypeStruct((m, n), out_dtype),
        compiler_params=pltpu.CompilerParams(
            dimension_semantics=("arbitrary", "arbitrary"), vmem_limit_bytes=VMEM_LIMIT),
        name="matmul",
    )(a, w)


def _mm_ksplit_kernel(a_ref, b_ref, o_ref):
    @pl.when(pl.program_id(2) == 0)
    def _():
        o_ref[...] = jnp.zeros_like(o_ref)

    o_ref[...] += jnp.dot(a_ref[...], b_ref[...], preferred_element_type=jnp.float32)


def _matmul_ksplit(a, w, l, tm, tn, tk):
    m, k = a.shape
    n = w.shape[2]
    tm = min(tm, m)
    assert m % tm == 0 and n % tn == 0 and k % tk == 0
    return pl.pallas_call(
        _mm_ksplit_kernel,
        grid=(m // tm, n // tn, k // tk),
        in_specs=[pl.BlockSpec((tm, tk), lambda i, j, kk: (i, kk)),
                  pl.BlockSpec((None, tk, tn), lambda i, j, kk: (l, kk, j))],
        out_specs=pl.BlockSpec((tm, tn), lambda i, j, kk: (i, j)),
        out_shape=jax.ShapeDtypeStruct((m, n), jnp.float32),
        compiler_params=pltpu.CompilerParams(
            dimension_semantics=("arbitrary", "arbitrary", "arbitrary"), vmem_limit_bytes=VMEM_LIMIT),
        name="matmul_ksplit",
    )(a, w)


def _merge_kernel(x_ref, wg_ref, h_ref, wb_ref, o_ref):
    n = pl.program_id(2)

    @pl.when(n == 0)
    def _():
        o_ref[...] = jnp.zeros_like(o_ref)

    gate = jax.nn.sigmoid(jnp.dot(x_ref[...], wg_ref[...], preferred_element_type=jnp.float32))
    y = jnp.dot(h_ref[0], wb_ref[0].astype(h_ref.dtype), preferred_element_type=jnp.float32)
    o_ref[...] += gate * y


def _gated_merge(x, w_gate, h, w_branch, l, tm, tn):
    nb, m, bw = h.shape
    d = w_branch.shape[-1]
    tm = min(tm, m)
    nj = d // tn
    assert m % tm == 0 and d % tn == 0
    return pl.pallas_call(
        _merge_kernel,
        grid=(m // tm, nj, nb),
        in_specs=[pl.BlockSpec((tm, d), lambda i, j, n: (i, 0)),
                  pl.BlockSpec((None, d, tn), lambda i, j, n: (l, 0, n * nj + j)),
                  pl.BlockSpec((1, tm, bw), lambda i, j, n: (n, i, 0)),
                  pl.BlockSpec((None, 1, bw, tn), lambda i, j, n: (l, n, 0, j))],
        out_specs=pl.BlockSpec((tm, tn), lambda i, j, n: (i, j)),
        out_shape=jax.ShapeDtypeStruct((m, d), jnp.float32),
        compiler_params=pltpu.CompilerParams(
            dimension_semantics=("arbitrary", "arbitrary", "arbitrary"), vmem_limit_bytes=VMEM_LIMIT),
        name="gated_merge",
    )(x, w_gate, h, w_branch)


LN_ROWS = 256


def _ln_kernel(x_ref, y_ref, w_ref, b_ref, o_ref, ob_ref):
    z = ALPHA * x_ref[...] + y_ref[...]
    mu = jnp.mean(z, axis=-1, keepdims=True)
    zc = z - mu
    var = jnp.mean(zc * zc, axis=-1, keepdims=True)
    o = zc * lax.rsqrt(var + LN_EPS) * w_ref[...] + b_ref[...]
    o_ref[...] = o
    ob_ref[...] = o.astype(ob_ref.dtype)


def _residual_layer_norm(x, y, w, b):
    m, d = x.shape
    tr = min(LN_ROWS, m)
    assert m % tr == 0
    row = pl.BlockSpec((tr, d), lambda i: (i, 0))
    vec = pl.BlockSpec((1, d), lambda i: (0, 0))
    return pl.pallas_call(
        _ln_kernel,
        grid=(m // tr,),
        in_specs=[row, row, vec, vec],
        out_specs=[row, row],
        out_shape=[jax.ShapeDtypeStruct((m, d), jnp.float32), jax.ShapeDtypeStruct((m, d), jnp.bfloat16)],
        compiler_params=pltpu.CompilerParams(dimension_semantics=("arbitrary",), vmem_limit_bytes=VMEM_LIMIT),
        name="residual_layer_norm",
    )(x, y, w.reshape(1, d), b.reshape(1, d))


FFN_ACT_COLS = D_FF // 2


def _ffn_act_kernel(u_ref, halo_ref, buf_ref, v_ref, wc_ref, o_ref, *, blocks_per_seq):
    i = pl.program_id(1)
    u = u_ref[...]
    first = (i % blocks_per_seq) == 0
    halo = halo_ref[...]
    prev = jnp.where(first, buf_ref[0], halo[SUBLANES - (CONV_W - 1):, :])
    ridx = lax.broadcasted_iota(jnp.int32, u.shape, 0)
    c = wc_ref[CONV_W - 1:CONV_W, :] * u
    for back in range(1, CONV_W):
        sh = pltpu.roll(u, back, axis=0)
        for j in range(back):
            sh = jnp.where(ridx == j, prev[CONV_W - 1 - back + j:CONV_W - back + j, :], sh)
        c = c + wc_ref[CONV_W - 1 - back:CONV_W - back, :] * sh
    o_ref[...] = (jax.nn.gelu(c) * v_ref[...]).astype(o_ref.dtype)


def _ffn_activation(up, buf, w_conv, seq_len, tr):
    m = up.shape[0]
    tr = min(tr, seq_len)
    tc = FFN_ACT_COLS
    ncb = D_FF // tc
    assert seq_len % tr == 0 and tr % SUBLANES == 0 and m % seq_len == 0
    bps = seq_len // tr
    hb = tr // SUBLANES
    return pl.pallas_call(
        functools.partial(_ffn_act_kernel, blocks_per_seq=bps),
        grid=(ncb, m // tr),
        in_specs=[pl.BlockSpec((tr, tc), lambda j, i: (i, j)),
                  pl.BlockSpec((SUBLANES, tc), lambda j, i: (jnp.maximum(i * hb - 1, 0), j)),
                  pl.BlockSpec((1, CONV_W - 1, tc), lambda j, i: (i // bps, 0, j)),
                  pl.BlockSpec((tr, tc), lambda j, i: (i, ncb + j)),
                  pl.BlockSpec((CONV_W, tc), lambda j, i: (0, j))],
        out_specs=pl.BlockSpec((tr, tc), lambda j, i: (i, j)),
        out_shape=jax.ShapeDtypeStruct((m, D_FF), jnp.bfloat16),
        compiler_params=pltpu.CompilerParams(
            dimension_semantics=("arbitrary", "arbitrary"), vmem_limit_bytes=VMEM_LIMIT),
        name="ffn_activation",
    )(up, up, buf, up, w_conv)


RW_TB = 64
RW_NK = 5


def _rw_rec_kernel(kx_ref, v_ref, s0_ref, y_ref, st_ref, s_scr, *, tb, vs):
    i = pl.program_id(0)

    @pl.when(i == 0)
    def _():
        s_scr[...] = s0_ref[...]

    def step(t, carry):
        def group(gi, c2):
            base = pl.multiple_of(gi * SUBLANES, SUBLANES)
            vrows = v_ref[t, pl.ds(base, SUBLANES), :]
            ys = []
            for j in range(SUBLANES):
                s = s_scr[base + j]
                sa = jnp.sum(s * kx_ref[t, 0], axis=0, keepdims=True)
                s = s * kx_ref[t, 2] + sa * kx_ref[t, 1] + vrows[j:j + 1, :] * kx_ref[t, 3]
                s_scr[base + j] = s
                ys.append(jnp.sum(s * kx_ref[t, 4], axis=0, keepdims=True))
            y_ref[t, pl.ds(base, SUBLANES), :] = jnp.concatenate(ys, axis=0)
            return c2

        lax.fori_loop(0, vs // SUBLANES, group, 0)
        return carry

    lax.fori_loop(0, tb, step, 0)

    @pl.when(i == pl.num_programs(0) - 1)
    def _():
        st_ref[...] = s_scr[...]


def _rwkv_recurrence(rh, k2h, vh, kkh, ah, dh, s0):
    b, t, h, hd = rh.shape
    fold = LANES // (b * h)
    vs = hd // fold
    tb = min(RW_TB, t)
    assert fold * b * h == LANES and t % tb == 0 and vs % SUBLANES == 0
    kx = jnp.stack([-kkh, kkh * ah, dh, k2h, rh])
    kx = kx.transpose(2, 0, 4, 1, 3).reshape(t, RW_NK, hd, b * h)
    kx = jnp.tile(kx, (1, 1, 1, fold))
    vv = vh.reshape(b, t, h, fold, vs).transpose(1, 4, 3, 0, 2).reshape(t, vs, LANES)
    s0l = s0.reshape(b, h, fold, vs, hd).transpose(3, 4, 2, 0, 1).reshape(vs, hd, LANES)
    y, st = pl.pallas_call(
        functools.partial(_rw_rec_kernel, tb=tb, vs=vs),
        grid=(t // tb,),
        in_specs=[pl.BlockSpec((tb, RW_NK, hd, LANES), lambda i: (i, 0, 0, 0)),
                  pl.BlockSpec((tb, vs, LANES), lambda i: (i, 0, 0)),
                  pl.BlockSpec((vs, hd, LANES), lambda i: (0, 0, 0))],
        out_specs=[pl.BlockSpec((tb, vs, LANES), lambda i: (i, 0, 0)),
                   pl.BlockSpec((vs, hd, LANES), lambda i: (0, 0, 0))],
        out_shape=[jax.ShapeDtypeStruct((t, vs, LANES), jnp.float32),
                   jax.ShapeDtypeStruct((vs, hd, LANES), jnp.float32)],
        scratch_shapes=[pltpu.VMEM((vs, hd, LANES), jnp.float32)],
        compiler_params=pltpu.CompilerParams(
            dimension_semantics=("arbitrary",), vmem_limit_bytes=VMEM_LIMIT),
        name="rwkv_recurrence",
    )(kx, vv, s0l)
    y = y.reshape(t, vs, fold, b, h).transpose(3, 0, 4, 2, 1).reshape(b, t, h, hd)
    st = st.reshape(vs, hd, fold, b, h).transpose(3, 4, 2, 0, 1).reshape(b, h, hd, hd)
    return y, st


HG_CHUNK_ROWS = 128
HG_SUB = 16
HG_BLOCK_ROWS = 256


def _hgrn_kernel(q_ref, f_ref, i_ref, g_ref, lb_ref, nw_ref, s0_ref, o_ref, st_ref, s_scr, *, n_valid, cdt):
    f32 = jnp.float32
    ch, sub = HG_CHUNK_ROWS, HG_SUB
    tb = pl.program_id(2)

    @pl.when(tb == 0)
    def _():
        s_scr[...] = s0_ref[0, 0]

    lb = lb_ref[...]
    log_lb = jnp.log(lb)
    log_1mlb = jnp.log1p(-lb)
    rows = q_ref.shape[0]
    tril = (lax.broadcasted_iota(jnp.int32, (ch, ch), 0) >= lax.broadcasted_iota(jnp.int32, (ch, ch), 1)).astype(f32)
    rowc = lax.broadcasted_iota(jnp.int32, (ch, HG_DK), 0)
    trow = lax.broadcasted_iota(jnp.int32, (sub, HG_DK), 0)
    lane = lax.broadcasted_iota(jnp.int32, (sub, ch), 1)
    nt = (((1,), (1,)), ((), ()))
    for c in range(rows // ch):
        sl = slice(c * ch, (c + 1) * ch)
        ff = f_ref[sl, :]
        log_sig = jnp.minimum(ff, 0.0) - jnp.log1p(jnp.exp(-jnp.abs(ff)))
        x2 = log_1mlb + log_sig
        log_f = jnp.maximum(log_lb, x2) + jnp.log1p(jnp.exp(-jnp.abs(log_lb - x2)))
        kk = (1.0 - lb) * jax.nn.sigmoid(-ff)
        if n_valid < rows:
            ok = rowc < (n_valid - c * ch)
            log_f = jnp.where(ok, log_f, 0.0)
            kk = jnp.where(ok, kk, 0.0)
        q = q_ref[sl, :]
        qq = q * jax.nn.sigmoid(q)
        vv = i_ref[sl, :]
        b = jnp.dot(tril, log_f, preferred_element_type=f32, precision=lax.Precision.HIGHEST)
        st = s_scr[...]
        inter = lax.dot_general((qq * jnp.exp(b)).astype(cdt), st.astype(cdt), nt, preferred_element_type=f32)
        strips = []
        for i in range(ch // sub):
            r0 = i * sub
            bi = b[r0:r0 + sub]
            qi = qq[r0:r0 + sub]
            ki = kk[r0:r0 + sub]
            if i > 0:
                bprev = b[r0 - 1:r0]
                ks = jnp.where(rowc < r0, kk * jnp.exp(jnp.minimum(bprev - b, 0.0)), 0.0)
                qs = qi * jnp.exp(bi - bprev)
                strip = lax.dot_general(qs.astype(cdt), ks.astype(cdt), nt, preferred_element_type=f32)
            else:
                strip = jnp.zeros((sub, ch), f32)
            for s in range(sub):
                keep = trow >= s
                e = jnp.exp(jnp.where(keep, bi - bi[s:s + 1, :], 0.0))
                col = jnp.sum(jnp.where(keep, qi * ki[s:s + 1, :] * e, 0.0), axis=-1, keepdims=True)
                strip = jnp.where(lane == r0 + s, col, strip)
            strips.append(strip)
        att = jnp.concatenate(strips, axis=0)
        o = inter + jnp.dot(att.astype(cdt), vv.astype(cdt), preferred_element_type=f32)
        bl = b[ch - 1:ch, :]
        kd = kk * jnp.exp(bl - b)
        s_scr[...] = st * jnp.exp(bl) + jnp.dot(vv.T.astype(cdt), kd.astype(cdt), preferred_element_type=f32)
        o = o * lax.rsqrt(jnp.mean(o * o, axis=-1, keepdims=True) + HG_EPS)
        g = g_ref[sl, :]
        o_ref[sl, :] = (o * nw_ref[...] * (g * jax.nn.sigmoid(g))).astype(o_ref.dtype)

    @pl.when(tb == pl.num_programs(2) - 1)
    def _():
        st_ref[0, 0] = s_scr[...]


def _hgrn2_pallas(p, s0, lb, norm_w, cdt=jnp.bfloat16, out_dtype=jnp.bfloat16):
    b, t, _ = p.shape
    assert HG_DK == LANES and HG_DV == LANES
    rows = min(HG_BLOCK_ROWS, -(-t // HG_CHUNK_ROWS) * HG_CHUNK_ROWS)
    tp = -(-t // rows) * rows
    n_valid = rows if tp == t else t
    assert tp == t or tp == rows
    if tp != t:
        p = jnp.pad(p, ((0, 0), (0, tp - t), (0, 0)))
    p2 = p.reshape(b * tp, HG_COLS)
    nt = tp // rows
    s0t = jnp.swapaxes(s0, 2, 3)
    col = lambda off: pl.BlockSpec((rows, LANES), lambda bi, hi, ti: (bi * nt + ti, off + hi))
    vec = pl.BlockSpec((1, LANES), lambda bi, hi, ti: (0, hi))
    sspec = pl.BlockSpec((1, 1, HG_DV, HG_DK), lambda bi, hi, ti: (bi, hi, 0, 0))
    o, st = pl.pallas_call(
        functools.partial(_hgrn_kernel, n_valid=n_valid, cdt=cdt),
        grid=(b, HG_H, nt),
        in_specs=[col(0), col(HG_H), col(2 * HG_H), col(3 * HG_H), vec, vec, sspec],
        out_specs=[pl.BlockSpec((rows, LANES), lambda bi, hi, ti: (bi * nt + ti, hi)), sspec],
        out_shape=[jax.ShapeDtypeStruct((b * tp, BRANCH_W), out_dtype),
                   jax.ShapeDtypeStruct((b, HG_H, HG_DV, HG_DK), jnp.float32)],
        scratch_shapes=[pltpu.VMEM((HG_DV, HG_DK), jnp.float32)],
        compiler_params=pltpu.CompilerParams(
            dimension_semantics=("arbitrary", "arbitrary", "arbitrary"), vmem_limit_bytes=VMEM_LIMIT),
        name="hgrn2",
    )(p2, p2, p2, p2, lb.reshape(1, HG_FDIM), norm_w.reshape(1, BRANCH_W), s0t)
    if tp != t:
        o = o.reshape(b, tp, BRANCH_W)[:, :t].reshape(b * t, BRANCH_W)
    return o, jnp.swapaxes(st, 2, 3)


NSA_KEY_STEP = 256


def _nsa_kernel(q_ref, kc_ref, vc_ref, ks_ref, vs_ref, kw_ref, vw_ref, gl_ref, ov_ref, e_ref, o_ref,
                *, n_keys, wb):
    f32 = jnp.float32
    cdt = q_ref.dtype
    qb = Q_BLOCK
    qi = pl.program_id(2)
    nt = (((1,), (1,)), ((), ()))
    tq = lax.broadcasted_iota(jnp.int32, (qb, 1), 0) + qi * qb

    def softmax_parts(s, mask):
        s = jnp.where(mask, s, NEG)
        e = jnp.exp(s - jnp.max(s, axis=-1, keepdims=True))
        return e, jnp.sum(e, axis=-1, keepdims=True)

    cidx = lax.broadcasted_iota(jnp.int32, (qb, LANES), 1)
    maskc = (cidx * CMP_STRIDE + (CMP_BLOCK - 1)) <= tq
    kc = kc_ref[0, 0]
    vc = vc_ref[0, 0]
    psum = jnp.zeros((qb, LANES), f32)
    o_cmp = []
    for r in range(NSA_R):
        q_r = q_ref[0, 0, 0, r * qb:(r + 1) * qb, :]
        s = lax.dot_general(q_r, kc, nt, preferred_element_type=f32) * NSA_SCALE
        e, l = softmax_parts(s, maskc)
        pr = jnp.where(maskc, e / l, 0.0)
        psum = psum + pr
        o_cmp.append(jnp.dot(pr.astype(cdt), vc, preferred_element_type=f32))

    imp = jnp.dot(psum, ov_ref[...], preferred_element_type=f32, precision=lax.Precision.HIGHEST)
    cur = lax.shift_right_logical(tq, int(math.log2(SEL_BLOCK)))
    sc = jnp.where(cidx <= cur, imp, -jnp.inf)
    sc = jnp.where(cidx == cur, jnp.inf, sc)
    ns = n_keys // SEL_BLOCK
    sct = sc.T[:ns]
    sidx = lax.broadcasted_iota(jnp.int32, (ns, qb), 0)
    rank = jnp.zeros((ns, qb), jnp.int32)
    for sp in range(ns):
        row = sct[sp:sp + 1, :]
        beats = (row > sct) | ((row == sct) & (sidx > sp))
        rank = rank + beats.astype(jnp.int32)
    selt = (rank < N_SEL).astype(f32)
    if ns < LANES:
        selt = jnp.concatenate([selt, jnp.zeros((LANES - ns, qb), f32)], axis=0)
    sel = selt.T.astype(jnp.bfloat16)

    kposw = lax.broadcasted_iota(jnp.int32, (qb, wb + qb), 1) + (qi * qb - wb)
    rel = tq - kposw
    maskw = (rel >= 0) & (rel < WINDOW) & (kposw >= 0)
    wstart = pl.multiple_of(qi * qb, qb)
    kw = kw_ref[0, 0, pl.ds(wstart, wb + qb), :]
    vw = vw_ref[0, 0, pl.ds(wstart, wb + qb), :]
    gates, o_cw = [], []
    for r in range(NSA_R):
        q_r = q_ref[0, 0, 0, r * qb:(r + 1) * qb, :]
        s = lax.dot_general(q_r, kw, nt, preferred_element_type=f32) * NSA_SCALE
        e, l = softmax_parts(s, maskw)
        o_win = jnp.dot(e.astype(cdt), vw, preferred_element_type=f32) / l
        g = jax.nn.sigmoid(gl_ref[0, 0, 0, r * qb:(r + 1) * qb, :])
        gates.append(g[:, 1:2])
        o_cw.append(g[:, 0:1] * o_cmp[r] + g[:, 2:3] * o_win)

    def finish(kext):
        msel = jnp.dot(sel, e_ref[:, :kext], preferred_element_type=f32)
        kpos = lax.broadcasted_iota(jnp.int32, (qb, kext), 1)
        valid = (msel > 0.5) & (kpos <= tq)
        ks = ks_ref[0, 0, :kext, :]
        vs = vs_ref[0, 0, :kext, :]
        outs = []
        for r in range(NSA_R):
            q_r = q_ref[0, 0, 0, r * qb:(r + 1) * qb, :]
            s = lax.dot_general(q_r, ks, nt, preferred_element_type=f32) * NSA_SCALE
            e, l = softmax_parts(s, valid)
            o_sel = jnp.dot(e.astype(cdt), vs, preferred_element_type=f32) / l
            outs.append(o_cw[r] + gates[r] * o_sel)
        o_ref[...] = jnp.concatenate(outs, axis=1).astype(o_ref.dtype)

    step = NSA_KEY_STEP if n_keys % NSA_KEY_STEP == 0 else n_keys
    for kext in range(step, n_keys + 1, step):
        @pl.when((qi >= (kext - step) // qb) & (qi < kext // qb))
        def _(kext=kext):
            finish(kext)


def _nsa_prompt(q, kc, vc, kv6, gl, win_buf, cdt, out_dtype):
    b, t, _ = q.shape
    qb = Q_BLOCK
    nqb = t // qb
    wb = win_buf.shape[1]
    nc = kc.shape[1]
    assert t % qb == 0 and nc <= LANES and t // SEL_BLOCK <= LANES and t // SEL_BLOCK >= N_SEL
    qa = q.reshape(b, nqb, qb, NSA_G, NSA_R, NSA_HD).transpose(0, 3, 1, 4, 2, 5)
    qa = qa.reshape(b, NSA_G, nqb, NSA_R * qb, NSA_HD).astype(cdt)
    gla = gl.reshape(b, nqb, qb, 3, NSA_G, NSA_R).transpose(0, 4, 1, 5, 2, 3).reshape(b, NSA_G, nqb, NSA_R * qb, 3)

    def bg(x):
        return x.transpose(0, 2, 1, 3).astype(cdt)

    pad = ((0, 0), (0, LANES - nc), (0, 0), (0, 0))
    kca, vca = bg(jnp.pad(kc, pad)), bg(jnp.pad(vc, pad))
    ksa, vsa = bg(kv6[:, :, 2]), bg(kv6[:, :, 3])
    kwa = bg(jnp.concatenate([win_buf[:, :, 0], kv6[:, :, 4]], axis=1))
    vwa = bg(jnp.concatenate([win_buf[:, :, 1], kv6[:, :, 5]], axis=1))
    ns = t // SEL_BLOCK
    ov = np.zeros((LANES, LANES), np.float32)
    cs = np.arange(nc)[:, None] * CMP_STRIDE
    ss = np.arange(ns)[None, :] * SEL_BLOCK
    ov[:nc, :ns] = (cs < ss + SEL_BLOCK) & (cs + CMP_BLOCK > ss)
    e = (np.arange(LANES)[:, None] == (np.arange(t)[None, :] // SEL_BLOCK)).astype(np.float32)
    kvspec = lambda n: pl.BlockSpec((1, 1, n, NSA_HD), lambda bi, gi, qi: (bi, gi, 0, 0))
    return pl.pallas_call(
        functools.partial(_nsa_kernel, n_keys=t, wb=wb),
        grid=(b, NSA_G, nqb),
        in_specs=[pl.BlockSpec((1, 1, 1, NSA_R * qb, NSA_HD), lambda bi, gi, qi: (bi, gi, qi, 0, 0)),
                  kvspec(LANES), kvspec(LANES), kvspec(t), kvspec(t), kvspec(wb + t), kvspec(wb + t),
                  pl.BlockSpec((1, 1, 1, NSA_R * qb, 3), lambda bi, gi, qi: (bi, gi, qi, 0, 0)),
                  pl.BlockSpec((LANES, LANES), lambda bi, gi, qi: (0, 0)),
                  pl.BlockSpec((LANES, t), lambda bi, gi, qi: (0, 0))],
        out_specs=pl.BlockSpec((qb, NSA_R * NSA_HD), lambda bi, gi, qi: (bi * nqb + qi, gi)),
        out_shape=jax.ShapeDtypeStruct((b * t, BRANCH_W), out_dtype),
        compiler_params=pltpu.CompilerParams(
            dimension_semantics=("arbitrary", "arbitrary", "arbitrary"), vmem_limit_bytes=VMEM_LIMIT),
        name="nsa_prompt",
    )(qa, kca, vca, ksa, vsa, kwa, vwa, gla, jnp.asarray(ov), jnp.asarray(e, jnp.bfloat16))


def _split(x, sizes):
    idx = tuple(int(s) for s in np.cumsum(sizes)[:-1])
    return jnp.split(x, idx, axis=-1)


def _rwkv7(p, shift_prev, S0, mu, w0, w2, a0, a2, g2, k_k, k_a, r_k, ln_w, ln_b):
    B, T, _ = p.shape
    f32 = jnp.float32
    p_prev = jnp.concatenate([shift_prev[:, None], p[:, :-1]], axis=1)
    xs = p + mu * (p_prev - p)
    r, k, v, wl, al, gl = _split(xs, (BRANCH_W, BRANCH_W, BRANCH_W, RW_W_RANK, RW_A_RANK, RW_G_RANK))
    w = -jax.nn.softplus(-(w0 + jnp.tanh(wl) @ w2).astype(f32)) - 0.5
    decay = jnp.exp(-jnp.exp(w))
    a = jax.nn.sigmoid((a0 + al @ a2).astype(f32))
    g = jax.nn.sigmoid(gl) @ g2

    def heads(t):
        return t.astype(f32).reshape(B, T, RW_H, RW_HD)

    kk = heads(k * k_k)
    kk = kk / jnp.maximum(jnp.linalg.norm(kk, axis=-1, keepdims=True), 1e-12)
    k2 = k.astype(f32) * (1.0 + (a - 1.0) * k_a)
    rh, kh, vh, ah, dh = heads(r), heads(k2), heads(v), heads(a), heads(decay)

    y, S_T = _rwkv_recurrence(rh, kh, vh, kk, ah, dh, S0.astype(f32))
    mean = y.mean(-1, keepdims=True)
    var = jnp.mean(jnp.square(y - mean), -1, keepdims=True)
    yn = ((y - mean) * lax.rsqrt(var + RW_GN_EPS)).reshape(B, T, BRANCH_W) * ln_w + ln_b
    bonus = (jnp.sum(rh * kh * r_k, -1, keepdims=True) * vh).reshape(B, T, BRANCH_W)
    out = (yn + bonus) * g
    return out, S_T, p[:, -1]


def _pool(p, buf, pos0, w_grp, scale):
    B, T, _ = p.shape
    xc = jnp.concatenate([buf, p], axis=1)
    cs = jnp.cumsum(xc.astype(jnp.float32), axis=1)
    cs = jnp.pad(cs, ((0, 0), (1, 0), (0, 0)))
    end = cs[:, POOL_BUF + 1: POOL_BUF + 1 + T]
    t = jnp.arange(T)
    outs = []
    for gi, w in enumerate(POOL_WINDOWS):
        lo, hi = gi * POOL_GW, (gi + 1) * POOL_GW
        start = cs[:, POOL_BUF + 1 - w: POOL_BUF + 1 - w + T, lo:hi]
        cnt = jnp.minimum(w, pos0 + t + 1).astype(jnp.float32)[None, :, None]
        outs.append((end[..., lo:hi] - start) / cnt - p[..., lo:hi].astype(jnp.float32))
    d = jnp.stack(outs, axis=2)
    y = jnp.einsum('btgc,gcd->btgd', d.astype(p.dtype), w_grp).reshape(B, T, BRANCH_W) * scale
    return y, xc[:, -POOL_BUF:]


def _compress(kv, pe, w1, w2):
    B, N = kv.shape[:2]
    nc = (N - CMP_BLOCK) // CMP_STRIDE + 1
    idx = jnp.arange(nc)[:, None] * CMP_STRIDE + jnp.arange(CMP_BLOCK)[None, :]
    blocks = kv[:, idx] + pe[None, None, :, None, :]
    blocks = blocks.transpose(0, 1, 3, 2, 4).reshape(B, nc, NSA_G, CMP_BLOCK * NSA_HD)
    return jax.nn.gelu(blocks @ w1) @ w2


def _overlap(nc, ns):
    cs = jnp.arange(nc)[:, None] * CMP_STRIDE
    ss = jnp.arange(ns)[None, :] * SEL_BLOCK
    return ((cs < ss + SEL_BLOCK) & (cs + CMP_BLOCK > ss)).astype(jnp.float32)


def _cmp_attention(qg, kc, vc, pos0):
    T = qg.shape[1]
    nc = kc.shape[1]
    s = jnp.einsum('btgrd,bcgd->bgrtc', qg, kc).astype(jnp.float32) * NSA_SCALE
    qpos = pos0 + jnp.arange(T)
    bend = jnp.arange(nc) * CMP_STRIDE + CMP_BLOCK - 1
    mask = bend[None, :] <= qpos[:, None]
    pr = jax.nn.softmax(jnp.where(mask, s, NEG), axis=-1) * mask
    o = jnp.einsum('bgrtc,bcgd->btgrd', pr.astype(vc.dtype), vc)
    return o, pr


def _sel_attention(qg, k, v, imp, pos0):
    B, T = qg.shape[:2]
    N = k.shape[1]
    ns = imp.shape[-1]
    n_sel = min(N_SEL, ns)
    padn = ns * SEL_BLOCK - N
    kb = jnp.pad(k, ((0, 0), (0, padn), (0, 0), (0, 0))).reshape(B, ns, SEL_BLOCK, NSA_G, NSA_HD).transpose(0, 3, 1, 2, 4)
    vb = jnp.pad(v, ((0, 0), (0, padn), (0, 0), (0, 0))).reshape(B, ns, SEL_BLOCK, NSA_G, NSA_HD).transpose(0, 3, 1, 2, 4)
    qpos = pos0 + jnp.arange(T)
    cur = qpos // SEL_BLOCK
    blk = jnp.arange(ns)
    sc = jnp.where(blk[None, :] <= cur[:, None], imp, -jnp.inf)
    sc = jnp.where(blk[None, :] == cur[:, None], jnp.inf, sc)
    _, idx = lax.top_k(sc, n_sel)
    Qb = min(Q_BLOCK, T)
    nqb = -(-T // Qb)
    Tp = nqb * Qb
    q_all = jnp.pad(qg, ((0, 0), (0, Tp - T), (0, 0), (0, 0), (0, 0))).reshape(B, nqb, Qb, NSA_G, NSA_R, NSA_HD).transpose(1, 0, 2, 3, 4, 5)
    idx_all = jnp.pad(idx, ((0, 0), (0, 0), (0, Tp - T), (0, 0))).reshape(B, NSA_G, nqb, Qb, n_sel).transpose(2, 0, 1, 3, 4)
    pos_all = (pos0 + jnp.arange(Tp)).reshape(nqb, Qb)
    bi = jnp.arange(B)[:, None, None, None]
    gi = jnp.arange(NSA_G)[None, :, None, None]
    lpos = jnp.arange(SEL_BLOCK)

    def block(inp):
        qb, ib, pb = inp
        kg = kb[bi, gi, ib]
        vg = vb[bi, gi, ib]
        s = jnp.einsum('bqgrd,bgqnld->bgrqnl', qb, kg).astype(jnp.float32) * NSA_SCALE
        kpos = ib[..., None] * SEL_BLOCK + lpos
        valid = kpos <= pb[None, None, :, None, None]
        s = jnp.where(valid[:, :, None], s, NEG)
        pr = jax.nn.softmax(s.reshape(B, NSA_G, NSA_R, Qb, n_sel * SEL_BLOCK), axis=-1).reshape(s.shape)
        return jnp.einsum('bgrqnl,bgqnld->bqgrd', pr.astype(vg.dtype), vg)

    o = lax.map(block, (q_all, idx_all, pos_all))
    return jnp.moveaxis(o, 0, 1).reshape(B, Tp, NSA_G, NSA_R, NSA_HD)[:, :T]


def _win_attention(qg, kv_new, kv_buf, pos0):
    B, T = qg.shape[:2]
    wb = kv_buf.shape[1]
    kv_cat = jnp.concatenate([kv_buf, kv_new], axis=1)
    Qb = min(Q_BLOCK, T)
    nqb = -(-T // Qb)
    Tp = nqb * Qb
    kv_pad = jnp.pad(kv_cat, ((0, 0), (0, Tp - T), (0, 0), (0, 0), (0, 0)))
    q_pad = jnp.pad(qg, ((0, 0), (0, Tp - T), (0, 0), (0, 0), (0, 0)))

    def block(c):
        start = c * Qb
        qs = lax.dynamic_slice_in_dim(q_pad, start, Qb, axis=1)
        kvs = lax.dynamic_slice_in_dim(kv_pad, start, wb + Qb, axis=1)
        qpos = pos0 + start + jnp.arange(Qb)
        kpos = pos0 - wb + start + jnp.arange(wb + Qb)
        rel = qpos[:, None] - kpos[None, :]
        mask = (rel >= 0) & (rel < WINDOW) & (kpos[None, :] >= 0)
        s = jnp.einsum('bqgrd,bkgd->bgrqk', qs, kvs[:, :, 0]).astype(jnp.float32) * NSA_SCALE
        pr = jax.nn.softmax(jnp.where(mask, s, NEG), axis=-1)
        return jnp.einsum('bgrqk,bkgd->bqgrd', pr.astype(kvs.dtype), kvs[:, :, 1])

    o = lax.map(block, jnp.arange(nqb))
    o = jnp.moveaxis(o, 0, 1).reshape(B, Tp, NSA_G, NSA_R, NSA_HD)[:, :T]
    return o, kv_cat


def _nsa(p, kv_past, win_buf, pos0, pe, w1, w2):
    B, T, _ = p.shape
    q, kv6, gl = _split(p, (BRANCH_W, NSA_KV_COLS, 3 * NSA_H))
    qg = q.reshape(B, T, NSA_G, NSA_R, NSA_HD)
    kv6 = kv6.reshape(B, T, 6, NSA_G, NSA_HD)
    kv_new = kv6[:, :, :4]
    kv_all = kv_new if kv_past is None else jnp.concatenate([kv_past, kv_new], axis=1)
    kc = _compress(kv_all[:, :, 0], pe[0], w1[0], w2[0])
    vc = _compress(kv_all[:, :, 1], pe[1], w1[1], w2[1])
    o_cmp, p_cmp = _cmp_attention(qg, kc, vc, pos0)
    ns = -(-kv_all.shape[1] // SEL_BLOCK)
    imp = jnp.einsum('bgrtc,cs->bgts', p_cmp, _overlap(kc.shape[1], ns))
    o_sel = _sel_attention(qg, kv_all[:, :, 2], kv_all[:, :, 3], imp, pos0)
    o_win, win_cat = _win_attention(qg, kv6[:, :, 4:], win_buf, pos0)
    g = jax.nn.sigmoid(gl.reshape(B, T, 3, NSA_G, NSA_R))[..., None]
    o = g[:, :, 0] * o_cmp + g[:, :, 1] * o_sel + g[:, :, 2] * o_win
    return o.reshape(B, T, BRANCH_W), kv_new, win_cat


def _nsa_nopast(p, win_buf, pe, w1, w2, cdt=jnp.bfloat16, out_dtype=jnp.bfloat16):
    B, T, _ = p.shape
    q, kv6, gl = _split(p, (BRANCH_W, NSA_KV_COLS, 3 * NSA_H))
    kv6 = kv6.reshape(B, T, 6, NSA_G, NSA_HD)
    kv_new = kv6[:, :, :4]
    kc = _compress(kv6[:, :, 0], pe[0], w1[0], w2[0])
    vc = _compress(kv6[:, :, 1], pe[1], w1[1], w2[1])
    o = _nsa_prompt(q, kc, vc, kv6, gl, win_buf, cdt, out_dtype)
    win_cat = jnp.concatenate([win_buf, kv6[:, :, 4:]], axis=1)
    return o, kv_new, win_cat


def _prep_weights(w_in, ffn_down):
    return _gate_weights(w_in), ffn_down.astype(jnp.bfloat16)


def _layer(l, xpair, pos0, kv_past, win_buf, rw_S, rw_shift, pool_buf, hg_S, ffn_buf, wts, prm, lb_all, win_keep):
    (w_in, w_gate_b, w_branch, w_out, ffn_up, ffn_down_b) = wts
    x, x2b = xpair
    B, T, _ = x.shape
    m = B * T
    bf = jnp.bfloat16
    x2 = x.reshape(m, D_MODEL)

    def in_proj(col0, n):
        return _matmul(x2b, w_in, l, 2048, IN_TN, col0, n)

    p_rw = in_proj(RW_OFF, RW_COLS).reshape(B, T, RW_COLS)
    p_pool = in_proj(POOL_OFF, BRANCH_W).reshape(B, T, BRANCH_W)
    p_hg = in_proj(HG_OFF, HG_COLS).reshape(B, T, HG_COLS)
    p_nsa = in_proj(NSA_OFF, NSA_COLS_PAD)[:, :NSA_COLS].reshape(B, T, NSA_COLS)
    h_rw, rw_S_new, rw_shift_new = _rwkv7(p_rw, rw_shift, rw_S, prm['rw_mu'][l], prm['rw_w0'][l], prm['rw_w2'][l],
                                          prm['rw_a0'][l], prm['rw_a2'][l], prm['rw_g2'][l], prm['rw_k_k'][l],
                                          prm['rw_k_a'][l], prm['rw_r_k'][l], prm['rw_ln_w'][l], prm['rw_ln_b'][l])
    h_pool, pool_new = _pool(p_pool, pool_buf, pos0, prm['pool_w'][l], prm['pool_scale'][l])
    h_hg, hg_S_new = _hgrn2_pallas(p_hg, hg_S.astype(jnp.float32), lb_all[l], prm['hg_norm_w'][l])
    if kv_past is None:
        h_nsa, kv_new, win_cat = _nsa_nopast(p_nsa, win_buf, prm['cmp_pe'][l], prm['cmp_w1'][l], prm['cmp_w2'][l])
    else:
        h_nsa, kv_new, win_cat = _nsa(p_nsa, kv_past, win_buf, pos0, prm['cmp_pe'][l], prm['cmp_w1'][l],
                                      prm['cmp_w2'][l])
    win_new = win_cat[:, -win_keep:]
    h = jnp.stack([t.reshape(m, BRANCH_W).astype(bf) for t in (h_rw, h_pool, h_hg, h_nsa)])
    merged = _gated_merge(x2b, w_gate_b, h, w_branch, l, 1024, 512)
    y = _matmul(merged.astype(bf), w_out, l, 1024, 512)
    x2, x2b = _residual_layer_norm(x2, y, prm['ln_w'][l, 0], prm['ln_b'][l, 0])
    up = _matmul(x2b, ffn_up, l, 1024, 512)
    hcv = _ffn_activation(up, ffn_buf, prm['ffn_conv'][l], T, 128)
    f = _matmul_ksplit(hcv, ffn_down_b, l, 1024, 512, D_FF // 2)
    x2, x2b = _residual_layer_norm(x2, f, prm['ln_w'][l, 1], prm['ln_b'][l, 1])
    assert T >= CONV_W - 1
    ffn_new = up.reshape(B, T, 2 * D_FF)[:, T - (CONV_W - 1):, :D_FF]
    return (x2.reshape(B, T, D_MODEL), x2b), (kv_new, win_new, rw_S_new, rw_shift_new, pool_new, hg_S_new, ffn_new)


def kernel(x_prompt, x_sample, cache_kv, page_table, state_win, state_rwkv, state_rwkv_shift, state_pool, state_hgrn, state_ffn_conv, w_in, rw_mu, rw_w0, rw_w2, rw_a0, rw_a2, rw_g2, rw_k_k, rw_k_a, rw_r_k, rw_ln_w, rw_ln_b, pool_w, pool_scale, hg_lb_raw, hg_norm_w, cmp_pe, cmp_w1, cmp_w2, w_branch, w_out, ln_w, ln_b, ffn_up, ffn_conv, ffn_down):
    prm = dict(rw_mu=rw_mu, rw_w0=rw_w0, rw_w2=rw_w2, rw_a0=rw_a0, rw_a2=rw_a2, rw_g2=rw_g2, rw_k_k=rw_k_k,
               rw_k_a=rw_k_a, rw_r_k=rw_r_k, rw_ln_w=rw_ln_w, rw_ln_b=rw_ln_b, pool_w=pool_w,
               pool_scale=pool_scale, hg_norm_w=hg_norm_w, cmp_pe=cmp_pe, cmp_w1=cmp_w1, cmp_w2=cmp_w2,
               ln_w=ln_w, ln_b=ln_b, ffn_conv=ffn_conv)
    lb_cum = jnp.cumsum(jax.nn.softmax(hg_lb_raw.astype(jnp.float32), axis=0), axis=0)
    lb_all = lb_cum - lb_cum[:1]
    past_len = page_table.shape[1] * cache_kv.shape[2]
    win_keep = state_win.shape[2]
    w_gate_b, ffn_down_b = _prep_weights(w_in, ffn_down)
    wts = (w_in, w_gate_b, w_branch, w_out, ffn_up, ffn_down_b)

    bp = x_prompt.shape[0]
    dt = x_prompt.dtype
    z_win = jnp.zeros((bp, WINDOW, 2, NSA_G, NSA_HD), dt)
    z_rw = jnp.zeros((bp, RW_H, RW_HD, RW_HD), dt)
    z_shift = jnp.zeros((bp, RW_COLS), dt)
    z_pool = jnp.zeros((bp, POOL_BUF, BRANCH_W), dt)
    z_hg = jnp.zeros((bp, HG_H, HG_DK, HG_DV), dt)
    z_ffn = jnp.zeros((bp, CONV_W - 1, D_FF), dt)
    xp = (x_prompt, x_prompt.reshape(-1, D_MODEL).astype(jnp.bfloat16))
    new_p = []
    for l in range(DEPTH):
        xp, st = _layer(l, xp, 0, None, z_win, z_rw, z_shift, z_pool, z_hg, z_ffn, wts, prm, lb_all, win_keep)
        new_p.append(st)

    db = x_sample.shape[0]
    xs = (x_sample, x_sample.reshape(-1, D_MODEL).astype(jnp.bfloat16))
    new_s = []
    for l in range(DEPTH):
        kv_past = cache_kv[l][page_table].reshape(db, past_len, 4, NSA_G, NSA_HD)
        xs, st = _layer(l, xs, past_len, kv_past, state_win[l], state_rwkv[l], state_rwkv_shift[l],
                        state_pool[l], state_hgrn[l], state_ffn_conv[l], wts, prm, lb_all, win_keep)
        new_s.append(st)

    sp = [jnp.stack([st[i] for st in new_p]) for i in range(7)]
    ss = [jnp.stack([st[i] for st in new_s]) for i in range(7)]
    return (xp[0], xs[0], sp[0], ss[0], sp[1], ss[1], sp[2], ss[2], sp[3], ss[3], sp[4], ss[4], sp[5], ss[5], sp[6], ss[6])
```

```python
import functools
import math

import jax
import jax.numpy as jnp
import numpy as np
from jax import lax
from jax.experimental import pallas as pl
from jax.experimental.pallas import tpu as pltpu

D_MODEL = 4096
DEPTH = 4
N_BRANCH = 4
BRANCH_W = D_MODEL // N_BRANCH
RW_HD = 64
RW_H = BRANCH_W // RW_HD
RW_W_RANK = 64
RW_A_RANK = 64
RW_G_RANK = 128
RW_COLS = 3 * BRANCH_W + RW_W_RANK + RW_A_RANK + RW_G_RANK
RW_GN_EPS = 64e-5
POOL_WINDOWS = (2, 4, 8, 16)
POOL_GW = BRANCH_W // len(POOL_WINDOWS)
POOL_BUF = max(POOL_WINDOWS) - 1
HG_DK = 128
HG_H = BRANCH_W // HG_DK
HG_DV = BRANCH_W // HG_H
HG_FDIM = HG_H * HG_DK
HG_COLS = 2 * HG_FDIM + 2 * BRANCH_W
HG_CHUNK = 64
HG_EPS = 1e-6
NSA_HD = 64
NSA_H = BRANCH_W // NSA_HD
NSA_G = 4
NSA_R = NSA_H // NSA_G
NSA_SCALE = NSA_HD ** -0.5
CMP_BLOCK = 32
CMP_STRIDE = 16
CMP_HIDDEN = 128
SEL_BLOCK = 64
N_SEL = 8
WINDOW = 512
Q_BLOCK = 128
NSA_KV_COLS = 6 * NSA_G * NSA_HD
NSA_COLS = BRANCH_W + NSA_KV_COLS + 3 * NSA_H
GATE_COLS = N_BRANCH * D_MODEL
IN_COLS = RW_COLS + BRANCH_W + HG_COLS + NSA_COLS + GATE_COLS
D_FF = 256 * ((8 * D_MODEL // 3 + 255) // 256)
CONV_W = 3
ALPHA = (2 * DEPTH) ** 0.25
LN_EPS = 1e-5
NEG = -1e30

RW_OFF = 0
POOL_OFF = RW_OFF + RW_COLS
HG_OFF = POOL_OFF + BRANCH_W
NSA_OFF = HG_OFF + HG_COLS
MIX_COLS = NSA_OFF + NSA_COLS
IN_TN = 256
NSA_COLS_PAD = -(-NSA_COLS // IN_TN) * IN_TN

VMEM_LIMIT = 56 * 1024 * 1024
LANES = 128
SUBLANES = 8


GATE_BLK0 = MIX_COLS // LANES
GATE_SHIFT = MIX_COLS % LANES
GATE_TN = 512
assert GATE_SHIFT % (2 * SUBLANES) == 0


def _gate_repack_kernel(*refs):
    o_ref = refs[-1]
    x = jnp.concatenate([r[...] for r in refs[:-1]], axis=0)
    o_ref[...] = x[GATE_SHIFT:GATE_SHIFT + GATE_TN, :].astype(o_ref.dtype)


def _gate_weights(w_in_t):
    nl, _, d = w_in_t.shape
    nin = GATE_TN // LANES + 1
    specs = [pl.BlockSpec((None, LANES, d),
                          lambda l, j, k=k: (l, GATE_BLK0 + (GATE_TN // LANES) * j + k, 0)) for k in range(nin)]
    return pl.pallas_call(
        _gate_repack_kernel,
        grid=(nl, GATE_COLS // GATE_TN),
        in_specs=specs,
        out_specs=pl.BlockSpec((None, GATE_TN, d), lambda l, j: (l, j, 0)),
        out_shape=jax.ShapeDtypeStruct((nl, GATE_COLS, d), jnp.bfloat16),
        compiler_params=pltpu.CompilerParams(
            dimension_semantics=("arbitrary", "arbitrary"), vmem_limit_bytes=VMEM_LIMIT),
        name="gate_repack",
    )(*([w_in_t] * nin))


NT_DIMS = (((1,), (1,)), ((), ()))


def _mm_kernel(a_ref, b_ref, o_ref):
    b = b_ref[...].astype(a_ref.dtype)
    o_ref[...] = jnp.dot(a_ref[...], b, preferred_element_type=jnp.float32).astype(o_ref.dtype)


def _mm_nt_kernel(a_ref, b_ref, o_ref):
    b = b_ref[...].astype(a_ref.dtype)
    o_ref[...] = lax.dot_general(a_ref[...], b, NT_DIMS, preferred_element_type=jnp.float32).astype(o_ref.dtype)


def _matmul(a, w, l, tm, tn, out_dtype=jnp.float32):
    m, k = a.shape
    n = w.shape[2]
    tm = min(tm, m)
    assert m % tm == 0 and n % tn == 0, (m, n, tm, tn)
    return pl.pallas_call(
        _mm_kernel,
        grid=(m // tm, n // tn),
        in_specs=[pl.BlockSpec((tm, k), lambda i, j: (i, 0)),
                  pl.BlockSpec((None, k, tn), lambda i, j: (l, 0, j))],
        out_specs=pl.BlockSpec((tm, tn), lambda i, j: (i, j)),
        out_shape=jax.ShapeDtypeStruct((m, n), out_dtype),
        compiler_params=pltpu.CompilerParams(
            dimension_semantics=("arbitrary", "arbitrary"), vmem_limit_bytes=VMEM_LIMIT),
        name="matmul",
    )(a, w)


def _matmul_nt(a, wt, l, tm, tn, row0, n, out_dtype=jnp.float32):
    m, k = a.shape
    tm = min(tm, m)
    assert m % tm == 0 and n % tn == 0 and row0 % tn == 0 and row0 + n <= wt.shape[1], (m, n, tm, tn, row0)
    r0 = row0 // tn
    return pl.pallas_call(
        _mm_nt_kernel,
        grid=(m // tm, n // tn),
        in_specs=[pl.BlockSpec((tm, k), lambda i, j: (i, 0)),
                  pl.BlockSpec((None, tn, k), lambda i, j: (l, r0 + j, 0))],
        out_specs=pl.BlockSpec((tm, tn), lambda i, j: (i, j)),
        out_shape=jax.ShapeDtypeStruct((m, n), out_dtype),
        compiler_params=pltpu.CompilerParams(
            dimension_semantics=("arbitrary", "arbitrary"), vmem_limit_bytes=VMEM_LIMIT),
        name="matmul_nt",
    )(a, wt)


def _mm_ksplit_kernel(a_ref, b_ref, o_ref):
    @pl.when(pl.program_id(2) == 0)
    def _():
        o_ref[...] = jnp.zeros_like(o_ref)

    o_ref[...] += jnp.dot(a_ref[...], b_ref[...], preferred_element_type=jnp.float32)


def _matmul_ksplit(a, w, l, tm, tn, tk):
    m, k = a.shape
    n = w.shape[2]
    tm = min(tm, m)
    assert m % tm == 0 and n % tn == 0 and k % tk == 0
    return pl.pallas_call(
        _mm_ksplit_kernel,
        grid=(m // tm, n // tn, k // tk),
        in_specs=[pl.BlockSpec((tm, tk), lambda i, j, kk: (i, kk)),
                  pl.BlockSpec((None, tk, tn), lambda i, j, kk: (l, kk, j))],
        out_specs=pl.BlockSpec((tm, tn), lambda i, j, kk: (i, j)),
        out_shape=jax.ShapeDtypeStruct((m, n), jnp.float32),
        compiler_params=pltpu.CompilerParams(
            dimension_semantics=("arbitrary", "arbitrary", "arbitrary"), vmem_limit_bytes=VMEM_LIMIT),
        name="matmul_ksplit",
    )(a, w)


def _merge_kernel(x_ref, wg_ref, h_ref, wb_ref, o_ref):
    n = pl.program_id(2)

    @pl.when(n == 0)
    def _():
        o_ref[...] = jnp.zeros_like(o_ref)

    gate = jax.nn.sigmoid(lax.dot_general(x_ref[...], wg_ref[...], NT_DIMS, preferred_element_type=jnp.float32))
    y = jnp.dot(h_ref[0], wb_ref[0].astype(h_ref.dtype), preferred_element_type=jnp.float32)
    o_ref[...] += gate * y


def _gated_merge(x, w_gate, h, w_branch, l, tm, tn):
    nb, m, bw = h.shape
    d = w_branch.shape[-1]
    tm = min(tm, m)
    nj = d // tn
    assert m % tm == 0 and d % tn == 0
    return pl.pallas_call(
        _merge_kernel,
        grid=(m // tm, nj, nb),
        in_specs=[pl.BlockSpec((tm, d), lambda i, j, n: (i, 0)),
                  pl.BlockSpec((None, tn, d), lambda i, j, n: (l, n * nj + j, 0)),
                  pl.BlockSpec((1, tm, bw), lambda i, j, n: (n, i, 0)),
                  pl.BlockSpec((None, 1, bw, tn), lambda i, j, n: (l, n, 0, j))],
        out_specs=pl.BlockSpec((tm, tn), lambda i, j, n: (i, j)),
        out_shape=jax.ShapeDtypeStruct((m, d), jnp.float32),
        compiler_params=pltpu.CompilerParams(
            dimension_semantics=("arbitrary", "arbitrary", "arbitrary"), vmem_limit_bytes=VMEM_LIMIT),
        name="gated_merge",
    )(x, w_gate, h, w_branch)


LN_ROWS = 256


def _ln_kernel(x_ref, y_ref, w_ref, b_ref, o_ref, ob_ref):
    z = ALPHA * x_ref[...] + y_ref[...]
    mu = jnp.mean(z, axis=-1, keepdims=True)
    zc = z - mu
    var = jnp.mean(zc * zc, axis=-1, keepdims=True)
    o = zc * lax.rsqrt(var + LN_EPS) * w_ref[...] + b_ref[...]
    o_ref[...] = o
    ob_ref[...] = o.astype(ob_ref.dtype)


def _residual_layer_norm(x, y, w, b):
    m, d = x.shape
    tr = min(LN_ROWS, m)
    assert m % tr == 0
    row = pl.BlockSpec((tr, d), lambda i: (i, 0))
    vec = pl.BlockSpec((1, d), lambda i: (0, 0))
    return pl.pallas_call(
        _ln_kernel,
        grid=(m // tr,),
        in_specs=[row, row, vec, vec],
        out_specs=[row, row],
        out_shape=[jax.ShapeDtypeStruct((m, d), jnp.float32), jax.ShapeDtypeStruct((m, d), jnp.bfloat16)],
        compiler_params=pltpu.CompilerParams(dimension_semantics=("arbitrary",), vmem_limit_bytes=VMEM_LIMIT),
        name="residual_layer_norm",
    )(x, y, w.reshape(1, d), b.reshape(1, d))


FFN_ACT_COLS = D_FF // 2


def _ffn_act_kernel(u_ref, halo_ref, buf_ref, v_ref, wc_ref, o_ref, *, blocks_per_seq):
    i = pl.program_id(1)
    u = u_ref[...]
    first = (i % blocks_per_seq) == 0
    halo = halo_ref[...]
    prev = jnp.where(first, buf_ref[0], halo[SUBLANES - (CONV_W - 1):, :])
    ridx = lax.broadcasted_iota(jnp.int32, u.shape, 0)
    c = wc_ref[CONV_W - 1:CONV_W, :] * u
    for back in range(1, CONV_W):
        sh = pltpu.roll(u, back, axis=0)
        for j in range(back):
            sh = jnp.where(ridx == j, prev[CONV_W - 1 - back + j:CONV_W - back + j, :], sh)
        c = c + wc_ref[CONV_W - 1 - back:CONV_W - back, :] * sh
    o_ref[...] = (jax.nn.gelu(c) * v_ref[...]).astype(o_ref.dtype)


def _ffn_activation(up, buf, w_conv, seq_len, tr):
    m = up.shape[0]
    tr = min(tr, seq_len)
    tc = FFN_ACT_COLS
    ncb = D_FF // tc
    assert seq_len % tr == 0 and tr % SUBLANES == 0 and m % seq_len == 0
    bps = seq_len // tr
    hb = tr // SUBLANES
    return pl.pallas_call(
        functools.partial(_ffn_act_kernel, blocks_per_seq=bps),
        grid=(ncb, m // tr),
        in_specs=[pl.BlockSpec((tr, tc), lambda j, i: (i, j)),
                  pl.BlockSpec((SUBLANES, tc), lambda j, i: (jnp.maximum(i * hb - 1, 0), j)),
                  pl.BlockSpec((1, CONV_W - 1, tc), lambda j, i: (i // bps, 0, j)),
                  pl.BlockSpec((tr, tc), lambda j, i: (i, ncb + j)),
                  pl.BlockSpec((CONV_W, tc), lambda j, i: (0, j))],
        out_specs=pl.BlockSpec((tr, tc), lambda j, i: (i, j)),
        out_shape=jax.ShapeDtypeStruct((m, D_FF), jnp.bfloat16),
        compiler_params=pltpu.CompilerParams(
            dimension_semantics=("arbitrary", "arbitrary"), vmem_limit_bytes=VMEM_LIMIT),
        name="ffn_activation",
    )(up, up, buf, up, w_conv)


RW_TB = 64
RW_NK = 5


def _rw_rec_kernel(kx_ref, v_ref, s0_ref, y_ref, st_ref, s_scr, *, tb, vs):
    i = pl.program_id(0)

    @pl.when(i == 0)
    def _():
        s_scr[...] = s0_ref[...]

    def step(t, carry):
        def group(gi, c2):
            base = pl.multiple_of(gi * SUBLANES, SUBLANES)
            vrows = v_ref[t, pl.ds(base, SUBLANES), :]
            ys = []
            for j in range(SUBLANES):
                s = s_scr[base + j]
                sa = jnp.sum(s * kx_ref[t, 0], axis=0, keepdims=True)
                s = s * kx_ref[t, 2] + sa * kx_ref[t, 1] + vrows[j:j + 1, :] * kx_ref[t, 3]
                s_scr[base + j] = s
                ys.append(jnp.sum(s * kx_ref[t, 4], axis=0, keepdims=True))
            y_ref[t, pl.ds(base, SUBLANES), :] = jnp.concatenate(ys, axis=0)
            return c2

        lax.fori_loop(0, vs // SUBLANES, group, 0)
        return carry

    lax.fori_loop(0, tb, step, 0)

    @pl.when(i == pl.num_programs(0) - 1)
    def _():
        st_ref[...] = s_scr[...]


def _rwkv_recurrence(rh, k2h, vh, kkh, ah, dh, s0):
    b, t, h, hd = rh.shape
    fold = LANES // (b * h)
    vs = hd // fold
    tb = min(RW_TB, t)
    assert fold * b * h == LANES and t % tb == 0 and vs % SUBLANES == 0
    kx = jnp.stack([-kkh, kkh * ah, dh, k2h, rh])
    kx = kx.transpose(2, 0, 4, 1, 3).reshape(t, RW_NK, hd, b * h)
    kx = jnp.tile(kx, (1, 1, 1, fold))
    vv = vh.reshape(b, t, h, fold, vs).transpose(1, 4, 3, 0, 2).reshape(t, vs, LANES)
    s0l = s0.reshape(b, h, fold, vs, hd).transpose(3, 4, 2, 0, 1).reshape(vs, hd, LANES)
    y, st = pl.pallas_call(
        functools.partial(_rw_rec_kernel, tb=tb, vs=vs),
        grid=(t // tb,),
        in_specs=[pl.BlockSpec((tb, RW_NK, hd, LANES), lambda i: (i, 0, 0, 0)),
                  pl.BlockSpec((tb, vs, LANES), lambda i: (i, 0, 0)),
                  pl.BlockSpec((vs, hd, LANES), lambda i: (0, 0, 0))],
        out_specs=[pl.BlockSpec((tb, vs, LANES), lambda i: (i, 0, 0)),
                   pl.BlockSpec((vs, hd, LANES), lambda i: (0, 0, 0))],
        out_shape=[jax.ShapeDtypeStruct((t, vs, LANES), jnp.float32),
                   jax.ShapeDtypeStruct((vs, hd, LANES), jnp.float32)],
        scratch_shapes=[pltpu.VMEM((vs, hd, LANES), jnp.float32)],
        compiler_params=pltpu.CompilerParams(
            dimension_semantics=("arbitrary",), vmem_limit_bytes=VMEM_LIMIT),
        name="rwkv_recurrence",
    )(kx, vv, s0l)
    y = y.reshape(t, vs, fold, b, h).transpose(3, 0, 4, 2, 1).reshape(b, t, h, hd)
    st = st.reshape(vs, hd, fold, b, h).transpose(3, 4, 2, 0, 1).reshape(b, h, hd, hd)
    return y, st


HG_CHUNK_ROWS = 128
HG_SUB = 16
HG_BLOCK_ROWS = 256


def _hgrn_kernel(q_ref, f_ref, i_ref, g_ref, lb_ref, nw_ref, s0_ref, o_ref, st_ref, s_scr, *, n_valid, cdt):
    f32 = jnp.float32
    ch, sub = HG_CHUNK_ROWS, HG_SUB
    tb = pl.program_id(2)

    @pl.when(tb == 0)
    def _():
        s_scr[...] = s0_ref[0, 0]

    lb = lb_ref[...]
    log_lb = jnp.log(lb)
    log_1mlb = jnp.log1p(-lb)
    rows = q_ref.shape[0]
    tril = (lax.broadcasted_iota(jnp.int32, (ch, ch), 0) >= lax.broadcasted_iota(jnp.int32, (ch, ch), 1)).astype(f32)
    rowc = lax.broadcasted_iota(jnp.int32, (ch, HG_DK), 0)
    trow = lax.broadcasted_iota(jnp.int32, (sub, HG_DK), 0)
    lane = lax.broadcasted_iota(jnp.int32, (sub, ch), 1)
    nt = (((1,), (1,)), ((), ()))
    for c in range(rows // ch):
        sl = slice(c * ch, (c + 1) * ch)
        ff = f_ref[sl, :]
        log_sig = jnp.minimum(ff, 0.0) - jnp.log1p(jnp.exp(-jnp.abs(ff)))
        x2 = log_1mlb + log_sig
        log_f = jnp.maximum(log_lb, x2) + jnp.log1p(jnp.exp(-jnp.abs(log_lb - x2)))
        kk = (1.0 - lb) * jax.nn.sigmoid(-ff)
        if n_valid < rows:
            ok = rowc < (n_valid - c * ch)
            log_f = jnp.where(ok, log_f, 0.0)
            kk = jnp.where(ok, kk, 0.0)
        q = q_ref[sl, :]
        qq = q * jax.nn.sigmoid(q)
        vv = i_ref[sl, :]
        b = jnp.dot(tril, log_f, preferred_element_type=f32, precision=lax.Precision.HIGHEST)
        st = s_scr[...]
        inter = lax.dot_general((qq * jnp.exp(b)).astype(cdt), st.astype(cdt), nt, preferred_element_type=f32)
        strips = []
        for i in range(ch // sub):
            r0 = i * sub
            bi = b[r0:r0 + sub]
            qi = qq[r0:r0 + sub]
            ki = kk[r0:r0 + sub]
            if i > 0:
                bprev = b[r0 - 1:r0]
                ks = jnp.where(rowc < r0, kk * jnp.exp(jnp.minimum(bprev - b, 0.0)), 0.0)
                qs = qi * jnp.exp(bi - bprev)
                strip = lax.dot_general(qs.astype(cdt), ks.astype(cdt), nt, preferred_element_type=f32)
            else:
                strip = jnp.zeros((sub, ch), f32)
            for s in range(sub):
                keep = trow >= s
                e = jnp.exp(jnp.where(keep, bi - bi[s:s + 1, :], 0.0))
                col = jnp.sum(jnp.where(keep, qi * ki[s:s + 1, :] * e, 0.0), axis=-1, keepdims=True)
                strip = jnp.where(lane == r0 + s, col, strip)
            strips.append(strip)
        att = jnp.concatenate(strips, axis=0)
        o = inter + jnp.dot(att.astype(cdt), vv.astype(cdt), preferred_element_type=f32)
        bl = b[ch - 1:ch, :]
        kd = kk * jnp.exp(bl - b)
        s_scr[...] = st * jnp.exp(bl) + jnp.dot(vv.T.astype(cdt), kd.astype(cdt), preferred_element_type=f32)
        o = o * lax.rsqrt(jnp.mean(o * o, axis=-1, keepdims=True) + HG_EPS)
        g = g_ref[sl, :]
        o_ref[sl, :] = (o * nw_ref[...] * (g * jax.nn.sigmoid(g))).astype(o_ref.dtype)

    @pl.when(tb == pl.num_programs(2) - 1)
    def _():
        st_ref[0, 0] = s_scr[...]


def _hgrn2_pallas(p, s0, lb, norm_w, cdt=jnp.bfloat16, out_dtype=jnp.bfloat16):
    b, t, _ = p.shape
    assert HG_DK == LANES and HG_DV == LANES
    rows = min(HG_BLOCK_ROWS, -(-t // HG_CHUNK_ROWS) * HG_CHUNK_ROWS)
    tp = -(-t // rows) * rows
    n_valid = rows if tp == t else t
    assert tp == t or tp == rows
    if tp != t:
        p = jnp.pad(p, ((0, 0), (0, tp - t), (0, 0)))
    p2 = p.reshape(b * tp, HG_COLS)
    nt = tp // rows
    s0t = jnp.swapaxes(s0, 2, 3)
    col = lambda off: pl.BlockSpec((rows, LANES), lambda bi, hi, ti: (bi * nt + ti, off + hi))
    vec = pl.BlockSpec((1, LANES), lambda bi, hi, ti: (0, hi))
    sspec = pl.BlockSpec((1, 1, HG_DV, HG_DK), lambda bi, hi, ti: (bi, hi, 0, 0))
    o, st = pl.pallas_call(
        functools.partial(_hgrn_kernel, n_valid=n_valid, cdt=cdt),
        grid=(b, HG_H, nt),
        in_specs=[col(0), col(HG_H), col(2 * HG_H), col(3 * HG_H), vec, vec, sspec],
        out_specs=[pl.BlockSpec((rows, LANES), lambda bi, hi, ti: (bi * nt + ti, hi)), sspec],
        out_shape=[jax.ShapeDtypeStruct((b * tp, BRANCH_W), out_dtype),
                   jax.ShapeDtypeStruct((b, HG_H, HG_DV, HG_DK), jnp.float32)],
        scratch_shapes=[pltpu.VMEM((HG_DV, HG_DK), jnp.float32)],
        compiler_params=pltpu.CompilerParams(
            dimension_semantics=("arbitrary", "arbitrary", "arbitrary"), vmem_limit_bytes=VMEM_LIMIT),
        name="hgrn2",
    )(p2, p2, p2, p2, lb.reshape(1, HG_FDIM), norm_w.reshape(1, BRANCH_W), s0t)
    if tp != t:
        o = o.reshape(b, tp, BRANCH_W)[:, :t].reshape(b * t, BRANCH_W)
    return o, jnp.swapaxes(st, 2, 3)


NSA_KEY_STEP = 256


def _nsa_kernel(q_ref, kc_ref, vc_ref, ks_ref, vs_ref, kw_ref, vw_ref, gl_ref, ov_ref, e_ref, o_ref,
                *, n_keys, wb):
    f32 = jnp.float32
    cdt = q_ref.dtype
    qb = Q_BLOCK
    qi = pl.program_id(2)
    nt = (((1,), (1,)), ((), ()))
    tq = lax.broadcasted_iota(jnp.int32, (qb, 1), 0) + qi * qb

    def softmax_parts(s, mask):
        s = jnp.where(mask, s, NEG)
        e = jnp.exp(s - jnp.max(s, axis=-1, keepdims=True))
        return e, jnp.sum(e, axis=-1, keepdims=True)

    cidx = lax.broadcasted_iota(jnp.int32, (qb, LANES), 1)
    maskc = (cidx * CMP_STRIDE + (CMP_BLOCK - 1)) <= tq
    kc = kc_ref[0, 0]
    vc = vc_ref[0, 0]
    psum = jnp.zeros((qb, LANES), f32)
    o_cmp = []
    for r in range(NSA_R):
        q_r = q_ref[0, 0, 0, r * qb:(r + 1) * qb, :]
        s = lax.dot_general(q_r, kc, nt, preferred_element_type=f32) * NSA_SCALE
        e, l = softmax_parts(s, maskc)
        pr = jnp.where(maskc, e / l, 0.0)
        psum = psum + pr
        o_cmp.append(jnp.dot(pr.astype(cdt), vc, preferred_element_type=f32))

    imp = jnp.dot(psum, ov_ref[...], preferred_element_type=f32, precision=lax.Precision.HIGHEST)
    cur = lax.shift_right_logical(tq, int(math.log2(SEL_BLOCK)))
    sc = jnp.where(cidx <= cur, imp, -jnp.inf)
    sc = jnp.where(cidx == cur, jnp.inf, sc)
    ns = n_keys // SEL_BLOCK
    sct = sc.T[:ns]
    sidx = lax.broadcasted_iota(jnp.int32, (ns, qb), 0)
    rank = jnp.zeros((ns, qb), jnp.int32)
    for sp in range(ns):
        row = sct[sp:sp + 1, :]
        beats = (row > sct) | ((row == sct) & (sidx > sp))
        rank = rank + beats.astype(jnp.int32)
    selt = (rank < N_SEL).astype(f32)
    if ns < LANES:
        selt = jnp.concatenate([selt, jnp.zeros((LANES - ns, qb), f32)], axis=0)
    sel = selt.T.astype(jnp.bfloat16)

    kposw = lax.broadcasted_iota(jnp.int32, (qb, wb + qb), 1) + (qi * qb - wb)
    rel = tq - kposw
    maskw = (rel >= 0) & (rel < WINDOW) & (kposw >= 0)
    wstart = pl.multiple_of(qi * qb, qb)
    kw = kw_ref[0, 0, pl.ds(wstart, wb + qb), :]
    vw = vw_ref[0, 0, pl.ds(wstart, wb + qb), :]
    gates, o_cw = [], []
    for r in range(NSA_R):
        q_r = q_ref[0, 0, 0, r * qb:(r + 1) * qb, :]
        s = lax.dot_general(q_r, kw, nt, preferred_element_type=f32) * NSA_SCALE
        e, l = softmax_parts(s, maskw)
        o_win = jnp.dot(e.astype(cdt), vw, preferred_element_type=f32) / l
        g = jax.nn.sigmoid(gl_ref[0, 0, 0, r * qb:(r + 1) * qb, :])
        gates.append(g[:, 1:2])
        o_cw.append(g[:, 0:1] * o_cmp[r] + g[:, 2:3] * o_win)

    def finish(kext):
        msel = jnp.dot(sel, e_ref[:, :kext], preferred_element_type=f32)
        kpos = lax.broadcasted_iota(jnp.int32, (qb, kext), 1)
        valid = (msel > 0.5) & (kpos <= tq)
        ks = ks_ref[0, 0, :kext, :]
        vs = vs_ref[0, 0, :kext, :]
        outs = []
        for r in range(NSA_R):
            q_r = q_ref[0, 0, 0, r * qb:(r + 1) * qb, :]
            s = lax.dot_general(q_r, ks, nt, preferred_element_type=f32) * NSA_SCALE
            e, l = softmax_parts(s, valid)
            o_sel = jnp.dot(e.astype(cdt), vs, preferred_element_type=f32) / l
            outs.append(o_cw[r] + gates[r] * o_sel)
        o_ref[...] = jnp.concatenate(outs, axis=1).astype(o_ref.dtype)

    step = NSA_KEY_STEP if n_keys % NSA_KEY_STEP == 0 else n_keys
    for kext in range(step, n_keys + 1, step):
        @pl.when((qi >= (kext - step) // qb) & (qi < kext // qb))
        def _(kext=kext):
            finish(kext)


def _nsa_prompt(q, kc, vc, kv6, gl, win_buf, cdt, out_dtype):
    b, t, _ = q.shape
    qb = Q_BLOCK
    nqb = t // qb
    wb = win_buf.shape[1]
    nc = kc.shape[1]
    assert t % qb == 0 and nc <= LANES and t // SEL_BLOCK <= LANES and t // SEL_BLOCK >= N_SEL
    qa = q.reshape(b, nqb, qb, NSA_G, NSA_R, NSA_HD).transpose(0, 3, 1, 4, 2, 5)
    qa = qa.reshape(b, NSA_G, nqb, NSA_R * qb, NSA_HD).astype(cdt)
    gla = gl.reshape(b, nqb, qb, 3, NSA_G, NSA_R).transpose(0, 4, 1, 5, 2, 3).reshape(b, NSA_G, nqb, NSA_R * qb, 3)

    def bg(x):
        return x.transpose(0, 2, 1, 3).astype(cdt)

    pad = ((0, 0), (0, LANES - nc), (0, 0), (0, 0))
    kca, vca = bg(jnp.pad(kc, pad)), bg(jnp.pad(vc, pad))
    ksa, vsa = bg(kv6[:, :, 2]), bg(kv6[:, :, 3])
    kwa = bg(jnp.concatenate([win_buf[:, :, 0], kv6[:, :, 4]], axis=1))
    vwa = bg(jnp.concatenate([win_buf[:, :, 1], kv6[:, :, 5]], axis=1))
    ns = t // SEL_BLOCK
    ov = np.zeros((LANES, LANES), np.float32)
    cs = np.arange(nc)[:, None] * CMP_STRIDE
    ss = np.arange(ns)[None, :] * SEL_BLOCK
    ov[:nc, :ns] = (cs < ss + SEL_BLOCK) & (cs + CMP_BLOCK > ss)
    e = (np.arange(LANES)[:, None] == (np.arange(t)[None, :] // SEL_BLOCK)).astype(np.float32)
    kvspec = lambda n: pl.BlockSpec((1, 1, n, NSA_HD), lambda bi, gi, qi: (bi, gi, 0, 0))
    return pl.pallas_call(
        functools.partial(_nsa_kernel, n_keys=t, wb=wb),
        grid=(b, NSA_G, nqb),
        in_specs=[pl.BlockSpec((1, 1, 1, NSA_R * qb, NSA_HD), lambda bi, gi, qi: (bi, gi, qi, 0, 0)),
                  kvspec(LANES), kvspec(LANES), kvspec(t), kvspec(t), kvspec(wb + t), kvspec(wb + t),
                  pl.BlockSpec((1, 1, 1, NSA_R * qb, 3), lambda bi, gi, qi: (bi, gi, qi, 0, 0)),
                  pl.BlockSpec((LANES, LANES), lambda bi, gi, qi: (0, 0)),
                  pl.BlockSpec((LANES, t), lambda bi, gi, qi: (0, 0))],
        out_specs=pl.BlockSpec((qb, NSA_R * NSA_HD), lambda bi, gi, qi: (bi * nqb + qi, gi)),
        out_shape=jax.ShapeDtypeStruct((b * t, BRANCH_W), out_dtype),
        compiler_params=pltpu.CompilerParams(
            dimension_semantics=("arbitrary", "arbitrary", "arbitrary"), vmem_limit_bytes=VMEM_LIMIT),
        name="nsa_prompt",
    )(qa, kca, vca, ksa, vsa, kwa, vwa, gla, jnp.asarray(ov), jnp.asarray(e, jnp.bfloat16))


def _split(x, sizes):
    idx = tuple(int(s) for s in np.cumsum(sizes)[:-1])
    return jnp.split(x, idx, axis=-1)


def _rwkv7(p, shift_prev, S0, mu, w0, w2, a0, a2, g2, k_k, k_a, r_k, ln_w, ln_b):
    B, T, _ = p.shape
    f32 = jnp.float32
    p_prev = jnp.concatenate([shift_prev[:, None], p[:, :-1]], axis=1)
    xs = p + mu * (p_prev - p)
    r, k, v, wl, al, gl = _split(xs, (BRANCH_W, BRANCH_W, BRANCH_W, RW_W_RANK, RW_A_RANK, RW_G_RANK))
    w = -jax.nn.softplus(-(w0 + jnp.tanh(wl) @ w2).astype(f32)) - 0.5
    decay = jnp.exp(-jnp.exp(w))
    a = jax.nn.sigmoid((a0 + al @ a2).astype(f32))
    g = jax.nn.sigmoid(gl) @ g2

    def heads(t):
        return t.astype(f32).reshape(B, T, RW_H, RW_HD)

    kk = heads(k * k_k)
    kk = kk / jnp.maximum(jnp.linalg.norm(kk, axis=-1, keepdims=True), 1e-12)
    k2 = k.astype(f32) * (1.0 + (a - 1.0) * k_a)
    rh, kh, vh, ah, dh = heads(r), heads(k2), heads(v), heads(a), heads(decay)

    y, S_T = _rwkv_recurrence(rh, kh, vh, kk, ah, dh, S0.astype(f32))
    mean = y.mean(-1, keepdims=True)
    var = jnp.mean(jnp.square(y - mean), -1, keepdims=True)
    yn = ((y - mean) * lax.rsqrt(var + RW_GN_EPS)).reshape(B, T, BRANCH_W) * ln_w + ln_b
    bonus = (jnp.sum(rh * kh * r_k, -1, keepdims=True) * vh).reshape(B, T, BRANCH_W)
    out = (yn + bonus) * g
    return out, S_T, p[:, -1]


def _pool(p, buf, pos0, w_grp, scale):
    B, T, _ = p.shape
    xc = jnp.concatenate([buf, p], axis=1)
    cs = jnp.cumsum(xc.astype(jnp.float32), axis=1)
    cs = jnp.pad(cs, ((0, 0), (1, 0), (0, 0)))
    end = cs[:, POOL_BUF + 1: POOL_BUF + 1 + T]
    t = jnp.arange(T)
    outs = []
    for gi, w in enumerate(POOL_WINDOWS):
        lo, hi = gi * POOL_GW, (gi + 1) * POOL_GW
        start = cs[:, POOL_BUF + 1 - w: POOL_BUF + 1 - w + T, lo:hi]
        cnt = jnp.minimum(w, pos0 + t + 1).astype(jnp.float32)[None, :, None]
        outs.append((end[..., lo:hi] - start) / cnt - p[..., lo:hi].astype(jnp.float32))
    d = jnp.stack(outs, axis=2)
    y = jnp.einsum('btgc,gcd->btgd', d.astype(p.dtype), w_grp).reshape(B, T, BRANCH_W) * scale
    return y, xc[:, -POOL_BUF:]


def _compress(kv, pe, w1, w2):
    B, N = kv.shape[:2]
    nc = (N - CMP_BLOCK) // CMP_STRIDE + 1
    idx = jnp.arange(nc)[:, None] * CMP_STRIDE + jnp.arange(CMP_BLOCK)[None, :]
    blocks = kv[:, idx] + pe[None, None, :, None, :]
    blocks = blocks.transpose(0, 1, 3, 2, 4).reshape(B, nc, NSA_G, CMP_BLOCK * NSA_HD)
    return jax.nn.gelu(blocks @ w1) @ w2


def _overlap(nc, ns):
    cs = jnp.arange(nc)[:, None] * CMP_STRIDE
    ss = jnp.arange(ns)[None, :] * SEL_BLOCK
    return ((cs < ss + SEL_BLOCK) & (cs + CMP_BLOCK > ss)).astype(jnp.float32)


def _cmp_attention(qg, kc, vc, pos0):
    T = qg.shape[1]
    nc = kc.shape[1]
    s = jnp.einsum('btgrd,bcgd->bgrtc', qg, kc).astype(jnp.float32) * NSA_SCALE
    qpos = pos0 + jnp.arange(T)
    bend = jnp.arange(nc) * CMP_STRIDE + CMP_BLOCK - 1
    mask = bend[None, :] <= qpos[:, None]
    pr = jax.nn.softmax(jnp.where(mask, s, NEG), axis=-1) * mask
    o = jnp.einsum('bgrtc,bcgd->btgrd', pr.astype(vc.dtype), vc)
    return o, pr


def _sel_attention(qg, k, v, imp, pos0):
    B, T = qg.shape[:2]
    N = k.shape[1]
    ns = imp.shape[-1]
    n_sel = min(N_SEL, ns)
    padn = ns * SEL_BLOCK - N
    kb = jnp.pad(k, ((0, 0), (0, padn), (0, 0), (0, 0))).reshape(B, ns, SEL_BLOCK, NSA_G, NSA_HD).transpose(0, 3, 1, 2, 4)
    vb = jnp.pad(v, ((0, 0), (0, padn), (0, 0), (0, 0))).reshape(B, ns, SEL_BLOCK, NSA_G, NSA_HD).transpose(0, 3, 1, 2, 4)
    qpos = pos0 + jnp.arange(T)
    cur = qpos // SEL_BLOCK
    blk = jnp.arange(ns)
    sc = jnp.where(blk[None, :] <= cur[:, None], imp, -jnp.inf)
    sc = jnp.where(blk[None, :] == cur[:, None], jnp.inf, sc)
    _, idx = lax.top_k(sc, n_sel)
    Qb = min(Q_BLOCK, T)
    nqb = -(-T // Qb)
    Tp = nqb * Qb
    q_all = jnp.pad(qg, ((0, 0), (0, Tp - T), (0, 0), (0, 0), (0, 0))).reshape(B, nqb, Qb, NSA_G, NSA_R, NSA_HD).transpose(1, 0, 2, 3, 4, 5)
    idx_all = jnp.pad(idx, ((0, 0), (0, 0), (0, Tp - T), (0, 0))).reshape(B, NSA_G, nqb, Qb, n_sel).transpose(2, 0, 1, 3, 4)
    pos_all = (pos0 + jnp.arange(Tp)).reshape(nqb, Qb)
    bi = jnp.arange(B)[:, None, None, None]
    gi = jnp.arange(NSA_G)[None, :, None, None]
    lpos = jnp.arange(SEL_BLOCK)

    def block(inp):
        qb, ib, pb = inp
        kg = kb[bi, gi, ib]
        vg = vb[bi, gi, ib]
        s = jnp.einsum('bqgrd,bgqnld->bgrqnl', qb, kg).astype(jnp.float32) * NSA_SCALE
        kpos = ib[..., None] * SEL_BLOCK + lpos
        valid = kpos <= pb[None, None, :, None, None]
        s = jnp.where(valid[:, :, None], s, NEG)
        pr = jax.nn.softmax(s.reshape(B, NSA_G, NSA_R, Qb, n_sel * SEL_BLOCK), axis=-1).reshape(s.shape)
        return jnp.einsum('bgrqnl,bgqnld->bqgrd', pr.astype(vg.dtype), vg)

    o = lax.map(block, (q_all, idx_all, pos_all))
    return jnp.moveaxis(o, 0, 1).reshape(B, Tp, NSA_G, NSA_R, NSA_HD)[:, :T]


def _win_attention(qg, kv_new, kv_buf, pos0):
    B, T = qg.shape[:2]
    wb = kv_buf.shape[1]
    kv_cat = jnp.concatenate([kv_buf, kv_new], axis=1)
    Qb = min(Q_BLOCK, T)
    nqb = -(-T // Qb)
    Tp = nqb * Qb
    kv_pad = jnp.pad(kv_cat, ((0, 0), (0, Tp - T), (0, 0), (0, 0), (0, 0)))
    q_pad = jnp.pad(qg, ((0, 0), (0, Tp - T), (0, 0), (0, 0), (0, 0)))

    def block(c):
        start = c * Qb
        qs = lax.dynamic_slice_in_dim(q_pad, start, Qb, axis=1)
        kvs = lax.dynamic_slice_in_dim(kv_pad, start, wb + Qb, axis=1)
        qpos = pos0 + start + jnp.arange(Qb)
        kpos = pos0 - wb + start + jnp.arange(wb + Qb)
        rel = qpos[:, None] - kpos[None, :]
        mask = (rel >= 0) & (rel < WINDOW) & (kpos[None, :] >= 0)
        s = jnp.einsum('bqgrd,bkgd->bgrqk', qs, kvs[:, :, 0]).astype(jnp.float32) * NSA_SCALE
        pr = jax.nn.softmax(jnp.where(mask, s, NEG), axis=-1)
        return jnp.einsum('bgrqk,bkgd->bqgrd', pr.astype(kvs.dtype), kvs[:, :, 1])

    o = lax.map(block, jnp.arange(nqb))
    o = jnp.moveaxis(o, 0, 1).reshape(B, Tp, NSA_G, NSA_R, NSA_HD)[:, :T]
    return o, kv_cat


def _nsa(p, kv_past, win_buf, pos0, pe, w1, w2):
    B, T, _ = p.shape
    q, kv6, gl = _split(p, (BRANCH_W, NSA_KV_COLS, 3 * NSA_H))
    qg = q.reshape(B, T, NSA_G, NSA_R, NSA_HD)
    kv6 = kv6.reshape(B, T, 6, NSA_G, NSA_HD)
    kv_new = kv6[:, :, :4]
    kv_all = kv_new if kv_past is None else jnp.concatenate([kv_past, kv_new], axis=1)
    kc = _compress(kv_all[:, :, 0], pe[0], w1[0], w2[0])
    vc = _compress(kv_all[:, :, 1], pe[1], w1[1], w2[1])
    o_cmp, p_cmp = _cmp_attention(qg, kc, vc, pos0)
    ns = -(-kv_all.shape[1] // SEL_BLOCK)
    imp = jnp.einsum('bgrtc,cs->bgts', p_cmp, _overlap(kc.shape[1], ns))
    o_sel = _sel_attention(qg, kv_all[:, :, 2], kv_all[:, :, 3], imp, pos0)
    o_win, win_cat = _win_attention(qg, kv6[:, :, 4:], win_buf, pos0)
    g = jax.nn.sigmoid(gl.reshape(B, T, 3, NSA_G, NSA_R))[..., None]
    o = g[:, :, 0] * o_cmp + g[:, :, 1] * o_sel + g[:, :, 2] * o_win
    return o.reshape(B, T, BRANCH_W), kv_new, win_cat


def _nsa_nopast(p, win_buf, pe, w1, w2, cdt=jnp.bfloat16, out_dtype=jnp.bfloat16):
    B, T, _ = p.shape
    q, kv6, gl = _split(p, (BRANCH_W, NSA_KV_COLS, 3 * NSA_H))
    kv6 = kv6.reshape(B, T, 6, NSA_G, NSA_HD)
    kv_new = kv6[:, :, :4]
    kc = _compress(kv6[:, :, 0], pe[0], w1[0], w2[0])
    vc = _compress(kv6[:, :, 1], pe[1], w1[1], w2[1])
    o = _nsa_prompt(q, kc, vc, kv6, gl, win_buf, cdt, out_dtype)
    win_cat = jnp.concatenate([win_buf, kv6[:, :, 4:]], axis=1)
    return o, kv_new, win_cat


def _prep_weights(w_in_t, ffn_down):
    return _gate_weights(w_in_t), ffn_down.astype(jnp.bfloat16)


def _layer(l, xpair, pos0, kv_past, win_buf, rw_S, rw_shift, pool_buf, hg_S, ffn_buf, wts, prm, lb_all, win_keep):
    (w_in_t, w_gate_b, w_branch, w_out, ffn_up, ffn_down_b) = wts
    x, x2b = xpair
    B, T, _ = x.shape
    m = B * T
    bf = jnp.bfloat16
    x2 = x.reshape(m, D_MODEL)

    def in_proj(col0, n):
        return _matmul_nt(x2b, w_in_t, l, 2048, IN_TN, col0, n)

    p_rw = in_proj(RW_OFF, RW_COLS).reshape(B, T, RW_COLS)
    p_pool = in_proj(POOL_OFF, BRANCH_W).reshape(B, T, BRANCH_W)
    p_hg = in_proj(HG_OFF, HG_COLS).reshape(B, T, HG_COLS)
    p_nsa = in_proj(NSA_OFF, NSA_COLS_PAD)[:, :NSA_COLS].reshape(B, T, NSA_COLS)
    h_rw, rw_S_new, rw_shift_new = _rwkv7(p_rw, rw_shift, rw_S, prm['rw_mu'][l], prm['rw_w0'][l], prm['rw_w2'][l],
                                          prm['rw_a0'][l], prm['rw_a2'][l], prm['rw_g2'][l], prm['rw_k_k'][l],
                                          prm['rw_k_a'][l], prm['rw_r_k'][l], prm['rw_ln_w'][l], prm['rw_ln_b'][l])
    h_pool, pool_new = _pool(p_pool, pool_buf, pos0, prm['pool_w'][l], prm['pool_scale'][l])
    h_hg, hg_S_new = _hgrn2_pallas(p_hg, hg_S.astype(jnp.float32), lb_all[l], prm['hg_norm_w'][l])
    if kv_past is None:
        h_nsa, kv_new, win_cat = _nsa_nopast(p_nsa, win_buf, prm['cmp_pe'][l], prm['cmp_w1'][l], prm['cmp_w2'][l])
    else:
        h_nsa, kv_new, win_cat = _nsa(p_nsa, kv_past, win_buf, pos0, prm['cmp_pe'][l], prm['cmp_w1'][l],
                                      prm['cmp_w2'][l])
    win_new = win_cat[:, -win_keep:]
    h = jnp.stack([t.reshape(m, BRANCH_W).astype(bf) for t in (h_rw, h_pool, h_hg, h_nsa)])
    merged = _gated_merge(x2b, w_gate_b, h, w_branch, l, 1024, 512)
    y = _matmul(merged.astype(bf), w_out, l, 1024, 512)
    x2, x2b = _residual_layer_norm(x2, y, prm['ln_w'][l, 0], prm['ln_b'][l, 0])
    up = _matmul(x2b, ffn_up, l, 1024, 512)
    hcv = _ffn_activation(up, ffn_buf, prm['ffn_conv'][l], T, 128)
    f = _matmul_ksplit(hcv, ffn_down_b, l, 1024, 512, D_FF // 2)
    x2, x2b = _residual_layer_norm(x2, f, prm['ln_w'][l, 1], prm['ln_b'][l, 1])
    assert T >= CONV_W - 1
    ffn_new = up.reshape(B, T, 2 * D_FF)[:, T - (CONV_W - 1):, :D_FF]
    return (x2.reshape(B, T, D_MODEL), x2b), (kv_new, win_new, rw_S_new, rw_shift_new, pool_new, hg_S_new, ffn_new)


def kernel(x_prompt, x_sample, cache_kv, page_table, state_win, state_rwkv, state_rwkv_shift, state_pool, state_hgrn, state_ffn_conv, w_in, rw_mu, rw_w0, rw_w2, rw_a0, rw_a2, rw_g2, rw_k_k, rw_k_a, rw_r_k, rw_ln_w, rw_ln_b, pool_w, pool_scale, hg_lb_raw, hg_norm_w, cmp_pe, cmp_w1, cmp_w2, w_branch, w_out, ln_w, ln_b, ffn_up, ffn_conv, ffn_down):
    prm = dict(rw_mu=rw_mu, rw_w0=rw_w0, rw_w2=rw_w2, rw_a0=rw_a0, rw_a2=rw_a2, rw_g2=rw_g2, rw_k_k=rw_k_k,
               rw_k_a=rw_k_a, rw_r_k=rw_r_k, rw_ln_w=rw_ln_w, rw_ln_b=rw_ln_b, pool_w=pool_w,
               pool_scale=pool_scale, hg_norm_w=hg_norm_w, cmp_pe=cmp_pe, cmp_w1=cmp_w1, cmp_w2=cmp_w2,
               ln_w=ln_w, ln_b=ln_b, ffn_conv=ffn_conv)
    lb_cum = jnp.cumsum(jax.nn.softmax(hg_lb_raw.astype(jnp.float32), axis=0), axis=0)
    lb_all = lb_cum - lb_cum[:1]
    past_len = page_table.shape[1] * cache_kv.shape[2]
    win_keep = state_win.shape[2]
    w_in_t = jnp.swapaxes(w_in, 1, 2)
    w_gate_b, ffn_down_b = _prep_weights(w_in_t, ffn_down)
    wts = (w_in_t, w_gate_b, w_branch, w_out, ffn_up, ffn_down_b)

    bp = x_prompt.shape[0]
    dt = x_prompt.dtype
    z_win = jnp.zeros((bp, WINDOW, 2, NSA_G, NSA_HD), dt)
    z_rw = jnp.zeros((bp, RW_H, RW_HD, RW_HD), dt)
    z_shift = jnp.zeros((bp, RW_COLS), dt)
    z_pool = jnp.zeros((bp, POOL_BUF, BRANCH_W), dt)
    z_hg = jnp.zeros((bp, HG_H, HG_DK, HG_DV), dt)
    z_ffn = jnp.zeros((bp, CONV_W - 1, D_FF), dt)
    xp = (x_prompt, x_prompt.reshape(-1, D_MODEL).astype(jnp.bfloat16))
    new_p = []
    for l in range(DEPTH):
        xp, st = _layer(l, xp, 0, None, z_win, z_rw, z_shift, z_pool, z_hg, z_ffn, wts, prm, lb_all, win_keep)
        new_p.append(st)

    db = x_sample.shape[0]
    xs = (x_sample, x_sample.reshape(-1, D_MODEL).astype(jnp.bfloat16))
    new_s = []
    for l in range(DEPTH):
        kv_past = cache_kv[l][page_table].reshape(db, past_len, 4, NSA_G, NSA_HD)
        xs, st = _layer(l, xs, past_len, kv_past, state_win[l], state_rwkv[l], state_rwkv_shift[l],
                        state_pool[l], state_hgrn[l], state_ffn_conv[l], wts, prm, lb_all, win_keep)
        new_s.append(st)

    sp = [jnp.stack([st[i] for st in new_p]) for i in range(7)]
    ss = [jnp.stack([st[i] for st in new_s]) for i in range(7)]
    return (xp[0], xs[0], sp[0], ss[0], sp[1], ss[1], sp[2], ss[2], sp[3], ss[3], sp[4], ss[4], sp[5], ss[5], sp[6], ss[6])
```

```python
import functools
import math

import jax
import jax.numpy as jnp
import numpy as np
from jax import lax
from jax.experimental import pallas as pl
from jax.experimental.pallas import tpu as pltpu

D_MODEL = 4096
DEPTH = 4
N_BRANCH = 4
BRANCH_W = D_MODEL // N_BRANCH
RW_HD = 64
RW_H = BRANCH_W // RW_HD
RW_W_RANK = 64
RW_A_RANK = 64
RW_G_RANK = 128
RW_COLS = 3 * BRANCH_W + RW_W_RANK + RW_A_RANK + RW_G_RANK
RW_GN_EPS = 64e-5
POOL_WINDOWS = (2, 4, 8, 16)
POOL_GW = BRANCH_W // len(POOL_WINDOWS)
POOL_BUF = max(POOL_WINDOWS) - 1
HG_DK = 128
HG_H = BRANCH_W // HG_DK
HG_DV = BRANCH_W // HG_H
HG_FDIM = HG_H * HG_DK
HG_COLS = 2 * HG_FDIM + 2 * BRANCH_W
HG_CHUNK = 64
HG_EPS = 1e-6
NSA_HD = 64
NSA_H = BRANCH_W // NSA_HD
NSA_G = 4
NSA_R = NSA_H // NSA_G
NSA_SCALE = NSA_HD ** -0.5
CMP_BLOCK = 32
CMP_STRIDE = 16
CMP_HIDDEN = 128
SEL_BLOCK = 64
N_SEL = 8
WINDOW = 512
Q_BLOCK = 128
NSA_KV_COLS = 6 * NSA_G * NSA_HD
NSA_COLS = BRANCH_W + NSA_KV_COLS + 3 * NSA_H
GATE_COLS = N_BRANCH * D_MODEL
IN_COLS = RW_COLS + BRANCH_W + HG_COLS + NSA_COLS + GATE_COLS
D_FF = 256 * ((8 * D_MODEL // 3 + 255) // 256)
CONV_W = 3
ALPHA = (2 * DEPTH) ** 0.25
LN_EPS = 1e-5
NEG = -1e30

RW_OFF = 0
POOL_OFF = RW_OFF + RW_COLS
HG_OFF = POOL_OFF + BRANCH_W
NSA_OFF = HG_OFF + HG_COLS
MIX_COLS = NSA_OFF + NSA_COLS
IN_TN = 256
NSA_COLS_PAD = -(-NSA_COLS // IN_TN) * IN_TN

VMEM_LIMIT = 56 * 1024 * 1024
LANES = 128
SUBLANES = 8


GATE_BLK0 = MIX_COLS // LANES
GATE_SHIFT = MIX_COLS % LANES
GATE_TN = 512
assert GATE_SHIFT % (2 * SUBLANES) == 0


def _gate_repack_kernel(*refs):
    o_ref = refs[-1]
    x = jnp.concatenate([r[...] for r in refs[:-1]], axis=0)
    o_ref[...] = x[GATE_SHIFT:GATE_SHIFT + GATE_TN, :].astype(o_ref.dtype)


def _gate_weights(w_in_t):
    nl, _, d = w_in_t.shape
    nin = GATE_TN // LANES + 1
    specs = [pl.BlockSpec((None, LANES, d),
                          lambda l, j, k=k: (l, GATE_BLK0 + (GATE_TN // LANES) * j + k, 0)) for k in range(nin)]
    return pl.pallas_call(
        _gate_repack_kernel,
        grid=(nl, GATE_COLS // GATE_TN),
        in_specs=specs,
        out_specs=pl.BlockSpec((None, GATE_TN, d), lambda l, j: (l, j, 0)),
        out_shape=jax.ShapeDtypeStruct((nl, GATE_COLS, d), jnp.bfloat16),
        compiler_params=pltpu.CompilerParams(
            dimension_semantics=("arbitrary", "arbitrary"), vmem_limit_bytes=VMEM_LIMIT),
        name="gate_repack",
    )(*([w_in_t] * nin))


NT_DIMS = (((1,), (1,)), ((), ()))


def _mm_kernel(a_ref, b_ref, o_ref):
    b = b_ref[...].astype(a_ref.dtype)
    o_ref[...] = jnp.dot(a_ref[...], b, preferred_element_type=jnp.float32).astype(o_ref.dtype)


def _mm_nt_kernel(a_ref, b_ref, o_ref):
    b = b_ref[...].astype(a_ref.dtype)
    o_ref[...] = lax.dot_general(a_ref[...], b, NT_DIMS, preferred_element_type=jnp.float32).astype(o_ref.dtype)


def _matmul(a, w, l, tm, tn, out_dtype=jnp.float32):
    m, k = a.shape
    n = w.shape[2]
    tm = min(tm, m)
    assert m % tm == 0 and n % tn == 0, (m, n, tm, tn)
    return pl.pallas_call(
        _mm_kernel,
        grid=(m // tm, n // tn),
        in_specs=[pl.BlockSpec((tm, k), lambda i, j: (i, 0)),
                  pl.BlockSpec((None, k, tn), lambda i, j: (l, 0, j))],
        out_specs=pl.BlockSpec((tm, tn), lambda i, j: (i, j)),
        out_shape=jax.ShapeDtypeStruct((m, n), out_dtype),
        compiler_params=pltpu.CompilerParams(
            dimension_semantics=("arbitrary", "arbitrary"), vmem_limit_bytes=VMEM_LIMIT),
        name="matmul",
    )(a, w)


def _matmul_nt(a, wt, l, tm, tn, row0, n, out_dtype=jnp.float32):
    m, k = a.shape
    tm = min(tm, m)
    assert m % tm == 0 and n % tn == 0 and row0 % tn == 0 and row0 + n <= wt.shape[1], (m, n, tm, tn, row0)
    r0 = row0 // tn
    return pl.pallas_call(
        _mm_nt_kernel,
        grid=(m // tm, n // tn),
        in_specs=[pl.BlockSpec((tm, k), lambda i, j: (i, 0)),
                  pl.BlockSpec((None, tn, k), lambda i, j: (l, r0 + j, 0))],
        out_specs=pl.BlockSpec((tm, tn), lambda i, j: (i, j)),
        out_shape=jax.ShapeDtypeStruct((m, n), out_dtype),
        compiler_params=pltpu.CompilerParams(
            dimension_semantics=("arbitrary", "arbitrary"), vmem_limit_bytes=VMEM_LIMIT),
        name="matmul_nt",
    )(a, wt)


def _mm_ksplit_kernel(a_ref, b_ref, o_ref):
    @pl.when(pl.program_id(2) == 0)
    def _():
        o_ref[...] = jnp.zeros_like(o_ref)

    o_ref[...] += jnp.dot(a_ref[...], b_ref[...], preferred_element_type=jnp.float32)


def _matmul_ksplit(a, w, l, tm, tn, tk):
    m, k = a.shape
    n = w.shape[2]
    tm = min(tm, m)
    assert m % tm == 0 and n % tn == 0 and k % tk == 0
    return pl.pallas_call(
        _mm_ksplit_kernel,
        grid=(m // tm, n // tn, k // tk),
        in_specs=[pl.BlockSpec((tm, tk), lambda i, j, kk: (i, kk)),
                  pl.BlockSpec((None, tk, tn), lambda i, j, kk: (l, kk, j))],
        out_specs=pl.BlockSpec((tm, tn), lambda i, j, kk: (i, j)),
        out_shape=jax.ShapeDtypeStruct((m, n), jnp.float32),
        compiler_params=pltpu.CompilerParams(
            dimension_semantics=("arbitrary", "arbitrary", "arbitrary"), vmem_limit_bytes=VMEM_LIMIT),
        name="matmul_ksplit",
    )(a, w)


def _merge_kernel(x_ref, wg_ref, h_ref, wb_ref, o_ref):
    n = pl.program_id(2)

    @pl.when(n == 0)
    def _():
        o_ref[...] = jnp.zeros_like(o_ref)

    gate = jax.nn.sigmoid(lax.dot_general(x_ref[...], wg_ref[...], NT_DIMS, preferred_element_type=jnp.float32))
    y = jnp.dot(h_ref[0], wb_ref[0].astype(h_ref.dtype), preferred_element_type=jnp.float32)
    o_ref[...] += gate * y


def _gated_merge(x, w_gate, h, w_branch, l, tm, tn):
    nb, m, bw = h.shape
    d = w_branch.shape[-1]
    tm = min(tm, m)
    nj = d // tn
    assert m % tm == 0 and d % tn == 0
    return pl.pallas_call(
        _merge_kernel,
        grid=(m // tm, nj, nb),
        in_specs=[pl.BlockSpec((tm, d), lambda i, j, n: (i, 0)),
                  pl.BlockSpec((None, tn, d), lambda i, j, n: (l, n * nj + j, 0)),
                  pl.BlockSpec((1, tm, bw), lambda i, j, n: (n, i, 0)),
                  pl.BlockSpec((None, 1, bw, tn), lambda i, j, n: (l, n, 0, j))],
        out_specs=pl.BlockSpec((tm, tn), lambda i, j, n: (i, j)),
        out_shape=jax.ShapeDtypeStruct((m, d), jnp.float32),
        compiler_params=pltpu.CompilerParams(
            dimension_semantics=("arbitrary", "arbitrary", "arbitrary"), vmem_limit_bytes=VMEM_LIMIT),
        name="gated_merge",
    )(x, w_gate, h, w_branch)


LN_ROWS = 256


def _ln_kernel(x_ref, y_ref, w_ref, b_ref, o_ref, ob_ref):
    z = ALPHA * x_ref[...] + y_ref[...]
    mu = jnp.mean(z, axis=-1, keepdims=True)
    zc = z - mu
    var = jnp.mean(zc * zc, axis=-1, keepdims=True)
    o = zc * lax.rsqrt(var + LN_EPS) * w_ref[...] + b_ref[...]
    o_ref[...] = o
    ob_ref[...] = o.astype(ob_ref.dtype)


def _residual_layer_norm(x, y, w, b):
    m, d = x.shape
    tr = min(LN_ROWS, m)
    assert m % tr == 0
    row = pl.BlockSpec((tr, d), lambda i: (i, 0))
    vec = pl.BlockSpec((1, d), lambda i: (0, 0))
    return pl.pallas_call(
        _ln_kernel,
        grid=(m // tr,),
        in_specs=[row, row, vec, vec],
        out_specs=[row, row],
        out_shape=[jax.ShapeDtypeStruct((m, d), jnp.float32), jax.ShapeDtypeStruct((m, d), jnp.bfloat16)],
        compiler_params=pltpu.CompilerParams(dimension_semantics=("arbitrary",), vmem_limit_bytes=VMEM_LIMIT),
        name="residual_layer_norm",
    )(x, y, w.reshape(1, d), b.reshape(1, d))


FFN_ACT_COLS = D_FF // 2


def _ffn_act_kernel(u_ref, halo_ref, buf_ref, v_ref, wc_ref, o_ref, *, blocks_per_seq):
    i = pl.program_id(1)
    u = u_ref[...]
    first = (i % blocks_per_seq) == 0
    halo = halo_ref[...]
    prev = jnp.where(first, buf_ref[0], halo[SUBLANES - (CONV_W - 1):, :])
    ridx = lax.broadcasted_iota(jnp.int32, u.shape, 0)
    c = wc_ref[CONV_W - 1:CONV_W, :] * u
    for back in range(1, CONV_W):
        sh = pltpu.roll(u, back, axis=0)
        for j in range(back):
            sh = jnp.where(ridx == j, prev[CONV_W - 1 - back + j:CONV_W - back + j, :], sh)
        c = c + wc_ref[CONV_W - 1 - back:CONV_W - back, :] * sh
    o_ref[...] = (jax.nn.gelu(c) * v_ref[...]).astype(o_ref.dtype)


def _ffn_activation(up, buf, w_conv, seq_len, tr):
    m = up.shape[0]
    tr = min(tr, seq_len)
    tc = FFN_ACT_COLS
    ncb = D_FF // tc
    assert seq_len % tr == 0 and tr % SUBLANES == 0 and m % seq_len == 0
    bps = seq_len // tr
    hb = tr // SUBLANES
    return pl.pallas_call(
        functools.partial(_ffn_act_kernel, blocks_per_seq=bps),
        grid=(ncb, m // tr),
        in_specs=[pl.BlockSpec((tr, tc), lambda j, i: (i, j)),
                  pl.BlockSpec((SUBLANES, tc), lambda j, i: (jnp.maximum(i * hb - 1, 0), j)),
                  pl.BlockSpec((1, CONV_W - 1, tc), lambda j, i: (i // bps, 0, j)),
                  pl.BlockSpec((tr, tc), lambda j, i: (i, ncb + j)),
                  pl.BlockSpec((CONV_W, tc), lambda j, i: (0, j))],
        out_specs=pl.BlockSpec((tr, tc), lambda j, i: (i, j)),
        out_shape=jax.ShapeDtypeStruct((m, D_FF), jnp.bfloat16),
        compiler_params=pltpu.CompilerParams(
            dimension_semantics=("arbitrary", "arbitrary"), vmem_limit_bytes=VMEM_LIMIT),
        name="ffn_activation",
    )(up, up, buf, up, w_conv)


RW_TB = 64
RW_NK = 5


def _rw_rec_kernel(kx_ref, v_ref, s0_ref, y_ref, st_ref, s_scr, kx_scr, *, tb, vs, fold):
    i = pl.program_id(0)

    @pl.when(i == 0)
    def _():
        s_scr[...] = s0_ref[...]

    def widen(t, carry):
        x = kx_ref[t]
        kx_scr[t] = x if fold == 1 else jnp.concatenate([x] * fold, axis=-1)
        return carry

    lax.fori_loop(0, tb, widen, 0, unroll=2)

    def step(t, carry):
        def group(gi, c2):
            base = pl.multiple_of(gi * SUBLANES, SUBLANES)
            vrows = v_ref[t, pl.ds(base, SUBLANES), :]
            ys = []
            for j in range(SUBLANES):
                s = s_scr[base + j]
                sa = jnp.sum(s * kx_scr[t, 0], axis=0, keepdims=True)
                s = s * kx_scr[t, 2] + sa * kx_scr[t, 1] + vrows[j:j + 1, :] * kx_scr[t, 3]
                s_scr[base + j] = s
                ys.append(jnp.sum(s * kx_scr[t, 4], axis=0, keepdims=True))
            y_ref[t, pl.ds(base, SUBLANES), :] = jnp.concatenate(ys, axis=0)
            return c2

        lax.fori_loop(0, vs // SUBLANES, group, 0)
        return carry

    lax.fori_loop(0, tb, step, 0)

    @pl.when(i == pl.num_programs(0) - 1)
    def _():
        st_ref[...] = s_scr[...]


def _rwkv_recurrence(rh, k2h, vh, kkh, ah, dh, s0):
    b, t, h, hd = rh.shape
    fold = LANES // (b * h)
    vs = hd // fold
    tb = min(RW_TB, t)
    assert fold * b * h == LANES and t % tb == 0 and vs % SUBLANES == 0
    kx = jnp.stack([-kkh, kkh * ah, dh, k2h, rh])
    kx = kx.transpose(2, 0, 4, 1, 3).reshape(t, RW_NK, hd, b * h)
    vv = vh.reshape(b, t, h, fold, vs).transpose(1, 4, 3, 0, 2).reshape(t, vs, LANES)
    s0l = s0.reshape(b, h, fold, vs, hd).transpose(3, 4, 2, 0, 1).reshape(vs, hd, LANES)
    y, st = pl.pallas_call(
        functools.partial(_rw_rec_kernel, tb=tb, vs=vs, fold=fold),
        grid=(t // tb,),
        in_specs=[pl.BlockSpec((tb, RW_NK, hd, b * h), lambda i: (i, 0, 0, 0)),
                  pl.BlockSpec((tb, vs, LANES), lambda i: (i, 0, 0)),
                  pl.BlockSpec((vs, hd, LANES), lambda i: (0, 0, 0))],
        out_specs=[pl.BlockSpec((tb, vs, LANES), lambda i: (i, 0, 0)),
                   pl.BlockSpec((vs, hd, LANES), lambda i: (0, 0, 0))],
        out_shape=[jax.ShapeDtypeStruct((t, vs, LANES), jnp.float32),
                   jax.ShapeDtypeStruct((vs, hd, LANES), jnp.float32)],
        scratch_shapes=[pltpu.VMEM((vs, hd, LANES), jnp.float32), pltpu.VMEM((tb, RW_NK, hd, LANES), jnp.float32)],
        compiler_params=pltpu.CompilerParams(
            dimension_semantics=("arbitrary",), vmem_limit_bytes=VMEM_LIMIT),
        name="rwkv_recurrence",
    )(kx, vv, s0l)
    y = y.reshape(t, vs, fold, b, h).transpose(3, 0, 4, 2, 1).reshape(b, t, h, hd)
    st = st.reshape(vs, hd, fold, b, h).transpose(3, 4, 2, 0, 1).reshape(b, h, hd, hd)
    return y, st


HG_CHUNK_ROWS = 128
HG_SUB = 16
HG_BLOCK_ROWS = 256


def _hgrn_kernel(q_ref, f_ref, i_ref, g_ref, lb_ref, nw_ref, s0_ref, o_ref, st_ref, s_scr, *, n_valid, cdt):
    f32 = jnp.float32
    ch, sub = HG_CHUNK_ROWS, HG_SUB
    tb = pl.program_id(2)

    @pl.when(tb == 0)
    def _():
        s_scr[...] = s0_ref[0, 0]

    lb = lb_ref[...]
    log_lb = jnp.log(lb)
    log_1mlb = jnp.log1p(-lb)
    rows = q_ref.shape[0]
    tril = (lax.broadcasted_iota(jnp.int32, (ch, ch), 0) >= lax.broadcasted_iota(jnp.int32, (ch, ch), 1)).astype(f32)
    rowc = lax.broadcasted_iota(jnp.int32, (ch, HG_DK), 0)
    trow = lax.broadcasted_iota(jnp.int32, (sub, HG_DK), 0)
    lane = lax.broadcasted_iota(jnp.int32, (sub, ch), 1)
    nt = (((1,), (1,)), ((), ()))
    for c in range(rows // ch):
        sl = slice(c * ch, (c + 1) * ch)
        ff = f_ref[sl, :]
        log_sig = jnp.minimum(ff, 0.0) - jnp.log1p(jnp.exp(-jnp.abs(ff)))
        x2 = log_1mlb + log_sig
        log_f = jnp.maximum(log_lb, x2) + jnp.log1p(jnp.exp(-jnp.abs(log_lb - x2)))
        kk = (1.0 - lb) * jax.nn.sigmoid(-ff)
        if n_valid < rows:
            ok = rowc < (n_valid - c * ch)
            log_f = jnp.where(ok, log_f, 0.0)
            kk = jnp.where(ok, kk, 0.0)
        q = q_ref[sl, :]
        qq = q * jax.nn.sigmoid(q)
        vv = i_ref[sl, :]
        b = jnp.dot(tril, log_f, preferred_element_type=f32, precision=lax.Precision.HIGHEST)
        st = s_scr[...]
        inter = lax.dot_general((qq * jnp.exp(b)).astype(cdt), st.astype(cdt), nt, preferred_element_type=f32)
        strips = []
        for i in range(ch // sub):
            r0 = i * sub
            bi = b[r0:r0 + sub]
            qi = qq[r0:r0 + sub]
            ki = kk[r0:r0 + sub]
            if i > 0:
                bprev = b[r0 - 1:r0]
                ks = jnp.where(rowc < r0, kk * jnp.exp(jnp.minimum(bprev - b, 0.0)), 0.0)
                qs = qi * jnp.exp(bi - bprev)
                strip = lax.dot_general(qs.astype(cdt), ks.astype(cdt), nt, preferred_element_type=f32)
            else:
                strip = jnp.zeros((sub, ch), f32)
            for s in range(sub):
                keep = trow >= s
                e = jnp.exp(jnp.where(keep, bi - bi[s:s + 1, :], 0.0))
                col = jnp.sum(jnp.where(keep, qi * ki[s:s + 1, :] * e, 0.0), axis=-1, keepdims=True)
                strip = jnp.where(lane == r0 + s, col, strip)
            strips.append(strip)
        att = jnp.concatenate(strips, axis=0)
        o = inter + jnp.dot(att.astype(cdt), vv.astype(cdt), preferred_element_type=f32)
        bl = b[ch - 1:ch, :]
        kd = kk * jnp.exp(bl - b)
        s_scr[...] = st * jnp.exp(bl) + jnp.dot(vv.T.astype(cdt), kd.astype(cdt), preferred_element_type=f32)
        o = o * lax.rsqrt(jnp.mean(o * o, axis=-1, keepdims=True) + HG_EPS)
        g = g_ref[sl, :]
        o_ref[sl, :] = (o * nw_ref[...] * (g * jax.nn.sigmoid(g))).astype(o_ref.dtype)

    @pl.when(tb == pl.num_programs(2) - 1)
    def _():
        st_ref[0, 0] = s_scr[...]


def _hgrn2_pallas(p, s0, lb, norm_w, cdt=jnp.bfloat16, out_dtype=jnp.bfloat16):
    b, t, _ = p.shape
    assert HG_DK == LANES and HG_DV == LANES
    rows = min(HG_BLOCK_ROWS, -(-t // HG_CHUNK_ROWS) * HG_CHUNK_ROWS)
    tp = -(-t // rows) * rows
    n_valid = rows if tp == t else t
    assert tp == t or tp == rows
    if tp != t:
        p = jnp.pad(p, ((0, 0), (0, tp - t), (0, 0)))
    p2 = p.reshape(b * tp, HG_COLS)
    nt = tp // rows
    s0t = jnp.swapaxes(s0, 2, 3)
    col = lambda off: pl.BlockSpec((rows, LANES), lambda bi, hi, ti: (bi * nt + ti, off + hi))
    vec = pl.BlockSpec((1, LANES), lambda bi, hi, ti: (0, hi))
    sspec = pl.BlockSpec((1, 1, HG_DV, HG_DK), lambda bi, hi, ti: (bi, hi, 0, 0))
    o, st = pl.pallas_call(
        functools.partial(_hgrn_kernel, n_valid=n_valid, cdt=cdt),
        grid=(b, HG_H, nt),
        in_specs=[col(0), col(HG_H), col(2 * HG_H), col(3 * HG_H), vec, vec, sspec],
        out_specs=[pl.BlockSpec((rows, LANES), lambda bi, hi, ti: (bi * nt + ti, hi)), sspec],
        out_shape=[jax.ShapeDtypeStruct((b * tp, BRANCH_W), out_dtype),
                   jax.ShapeDtypeStruct((b, HG_H, HG_DV, HG_DK), jnp.float32)],
        scratch_shapes=[pltpu.VMEM((HG_DV, HG_DK), jnp.float32)],
        compiler_params=pltpu.CompilerParams(
            dimension_semantics=("arbitrary", "arbitrary", "arbitrary"), vmem_limit_bytes=VMEM_LIMIT),
        name="hgrn2",
    )(p2, p2, p2, p2, lb.reshape(1, HG_FDIM), norm_w.reshape(1, BRANCH_W), s0t)
    if tp != t:
        o = o.reshape(b, tp, BRANCH_W)[:, :t].reshape(b * t, BRANCH_W)
    return o, jnp.swapaxes(st, 2, 3)


NSA_KEY_STEP = 256


def _nsa_kernel(q_ref, kc_ref, vc_ref, ks_ref, vs_ref, kw_ref, vw_ref, gl_ref, ov_ref, e_ref, o_ref,
                *, n_keys, wb):
    f32 = jnp.float32
    cdt = q_ref.dtype
    qb = Q_BLOCK
    qi = pl.program_id(2)
    nt = (((1,), (1,)), ((), ()))
    tq = lax.broadcasted_iota(jnp.int32, (qb, 1), 0) + qi * qb

    def softmax_parts(s, mask):
        s = jnp.where(mask, s, NEG)
        e = jnp.exp(s - jnp.max(s, axis=-1, keepdims=True))
        return e, jnp.sum(e, axis=-1, keepdims=True)

    cidx = lax.broadcasted_iota(jnp.int32, (qb, LANES), 1)
    maskc = (cidx * CMP_STRIDE + (CMP_BLOCK - 1)) <= tq
    kc = kc_ref[0, 0]
    vc = vc_ref[0, 0]
    psum = jnp.zeros((qb, LANES), f32)
    o_cmp = []
    for r in range(NSA_R):
        q_r = q_ref[0, 0, 0, r * qb:(r + 1) * qb, :]
        s = lax.dot_general(q_r, kc, nt, preferred_element_type=f32) * NSA_SCALE
        e, l = softmax_parts(s, maskc)
        pr = jnp.where(maskc, e / l, 0.0)
        psum = psum + pr
        o_cmp.append(jnp.dot(pr.astype(cdt), vc, preferred_element_type=f32))

    imp = jnp.dot(psum, ov_ref[...], preferred_element_type=f32, precision=lax.Precision.HIGHEST)
    cur = lax.shift_right_logical(tq, int(math.log2(SEL_BLOCK)))
    sc = jnp.where(cidx <= cur, imp, -jnp.inf)
    sc = jnp.where(cidx == cur, jnp.inf, sc)
    ns = n_keys // SEL_BLOCK
    sct = sc.T[:ns]
    sidx = lax.broadcasted_iota(jnp.int32, (ns, qb), 0)
    rank = jnp.zeros((ns, qb), jnp.int32)
    for sp in range(ns):
        row = sct[sp:sp + 1, :]
        beats = (row > sct) | ((row == sct) & (sidx > sp))
        rank = rank + beats.astype(jnp.int32)
    selt = (rank < N_SEL).astype(f32)
    if ns < LANES:
        selt = jnp.concatenate([selt, jnp.zeros((LANES - ns, qb), f32)], axis=0)
    sel = selt.T.astype(jnp.bfloat16)

    kposw = lax.broadcasted_iota(jnp.int32, (qb, wb + qb), 1) + (qi * qb - wb)
    rel = tq - kposw
    maskw = (rel >= 0) & (rel < WINDOW) & (kposw >= 0)
    wstart = pl.multiple_of(qi * qb, qb)
    kw = kw_ref[0, 0, pl.ds(wstart, wb + qb), :]
    vw = vw_ref[0, 0, pl.ds(wstart, wb + qb), :]
    gates, o_cw = [], []
    for r in range(NSA_R):
        q_r = q_ref[0, 0, 0, r * qb:(r + 1) * qb, :]
        s = lax.dot_general(q_r, kw, nt, preferred_element_type=f32) * NSA_SCALE
        e, l = softmax_parts(s, maskw)
        o_win = jnp.dot(e.astype(cdt), vw, preferred_element_type=f32) / l
        g = jax.nn.sigmoid(gl_ref[0, 0, 0, r * qb:(r + 1) * qb, :])
        gates.append(g[:, 1:2])
        o_cw.append(g[:, 0:1] * o_cmp[r] + g[:, 2:3] * o_win)

    def finish(kext):
        msel = jnp.dot(sel, e_ref[:, :kext], preferred_element_type=f32)
        kpos = lax.broadcasted_iota(jnp.int32, (qb, kext), 1)
        valid = (msel > 0.5) & (kpos <= tq)
        ks = ks_ref[0, 0, :kext, :]
        vs = vs_ref[0, 0, :kext, :]
        outs = []
        for r in range(NSA_R):
            q_r = q_ref[0, 0, 0, r * qb:(r + 1) * qb, :]
            s = lax.dot_general(q_r, ks, nt, preferred_element_type=f32) * NSA_SCALE
            e, l = softmax_parts(s, valid)
            o_sel = jnp.dot(e.astype(cdt), vs, preferred_element_type=f32) / l
            outs.append(o_cw[r] + gates[r] * o_sel)
        o_ref[...] = jnp.concatenate(outs, axis=1).astype(o_ref.dtype)

    step = NSA_KEY_STEP if n_keys % NSA_KEY_STEP == 0 else n_keys
    for kext in range(step, n_keys + 1, step):
        @pl.when((qi >= (kext - step) // qb) & (qi < kext // qb))
        def _(kext=kext):
            finish(kext)


def _nsa_prompt(q, kc, vc, kv6, gl, win_buf, cdt, out_dtype):
    b, t, _ = q.shape
    qb = Q_BLOCK
    nqb = t // qb
    wb = win_buf.shape[1]
    nc = kc.shape[1]
    assert t % qb == 0 and nc <= LANES and t // SEL_BLOCK <= LANES and t // SEL_BLOCK >= N_SEL
    qa = q.reshape(b, nqb, qb, NSA_G, NSA_R, NSA_HD).transpose(0, 3, 1, 4, 2, 5)
    qa = qa.reshape(b, NSA_G, nqb, NSA_R * qb, NSA_HD).astype(cdt)
    gla = gl.reshape(b, nqb, qb, 3, NSA_G, NSA_R).transpose(0, 4, 1, 5, 2, 3).reshape(b, NSA_G, nqb, NSA_R * qb, 3)

    def bg(x):
        return x.transpose(0, 2, 1, 3).astype(cdt)

    pad = ((0, 0), (0, LANES - nc), (0, 0), (0, 0))
    kca, vca = bg(jnp.pad(kc, pad)), bg(jnp.pad(vc, pad))
    ksa, vsa = bg(kv6[:, :, 2]), bg(kv6[:, :, 3])
    kwa = bg(jnp.concatenate([win_buf[:, :, 0], kv6[:, :, 4]], axis=1))
    vwa = bg(jnp.concatenate([win_buf[:, :, 1], kv6[:, :, 5]], axis=1))
    ns = t // SEL_BLOCK
    ov = np.zeros((LANES, LANES), np.float32)
    cs = np.arange(nc)[:, None] * CMP_STRIDE
    ss = np.arange(ns)[None, :] * SEL_BLOCK
    ov[:nc, :ns] = (cs < ss + SEL_BLOCK) & (cs + CMP_BLOCK > ss)
    e = (np.arange(LANES)[:, None] == (np.arange(t)[None, :] // SEL_BLOCK)).astype(np.float32)
    kvspec = lambda n: pl.BlockSpec((1, 1, n, NSA_HD), lambda bi, gi, qi: (bi, gi, 0, 0))
    return pl.pallas_call(
        functools.partial(_nsa_kernel, n_keys=t, wb=wb),
        grid=(b, NSA_G, nqb),
        in_specs=[pl.BlockSpec((1, 1, 1, NSA_R * qb, NSA_HD), lambda bi, gi, qi: (bi, gi, qi, 0, 0)),
                  kvspec(LANES), kvspec(LANES), kvspec(t), kvspec(t), kvspec(wb + t), kvspec(wb + t),
                  pl.BlockSpec((1, 1, 1, NSA_R * qb, 3), lambda bi, gi, qi: (bi, gi, qi, 0, 0)),
                  pl.BlockSpec((LANES, LANES), lambda bi, gi, qi: (0, 0)),
                  pl.BlockSpec((LANES, t), lambda bi, gi, qi: (0, 0))],
        out_specs=pl.BlockSpec((qb, NSA_R * NSA_HD), lambda bi, gi, qi: (bi * nqb + qi, gi)),
        out_shape=jax.ShapeDtypeStruct((b * t, BRANCH_W), out_dtype),
        compiler_params=pltpu.CompilerParams(
            dimension_semantics=("arbitrary", "arbitrary", "arbitrary"), vmem_limit_bytes=VMEM_LIMIT),
        name="nsa_prompt",
    )(qa, kca, vca, ksa, vsa, kwa, vwa, gla, jnp.asarray(ov), jnp.asarray(e, jnp.bfloat16))


def _split(x, sizes):
    idx = tuple(int(s) for s in np.cumsum(sizes)[:-1])
    return jnp.split(x, idx, axis=-1)


def _rwkv7(p, shift_prev, S0, mu, w0, w2, a0, a2, g2, k_k, k_a, r_k, ln_w, ln_b):
    B, T, _ = p.shape
    f32 = jnp.float32
    p_prev = jnp.concatenate([shift_prev[:, None], p[:, :-1]], axis=1)
    xs = p + mu * (p_prev - p)
    r, k, v, wl, al, gl = _split(xs, (BRANCH_W, BRANCH_W, BRANCH_W, RW_W_RANK, RW_A_RANK, RW_G_RANK))
    w = -jax.nn.softplus(-(w0 + jnp.tanh(wl) @ w2).astype(f32)) - 0.5
    decay = jnp.exp(-jnp.exp(w))
    a = jax.nn.sigmoid((a0 + al @ a2).astype(f32))
    g = jax.nn.sigmoid(gl) @ g2

    def heads(t):
        return t.astype(f32).reshape(B, T, RW_H, RW_HD)

    kk = heads(k * k_k)
    kk = kk / jnp.maximum(jnp.linalg.norm(kk, axis=-1, keepdims=True), 1e-12)
    k2 = k.astype(f32) * (1.0 + (a - 1.0) * k_a)
    rh, kh, vh, ah, dh = heads(r), heads(k2), heads(v), heads(a), heads(decay)

    y, S_T = _rwkv_recurrence(rh, kh, vh, kk, ah, dh, S0.astype(f32))
    mean = y.mean(-1, keepdims=True)
    var = jnp.mean(jnp.square(y - mean), -1, keepdims=True)
    yn = ((y - mean) * lax.rsqrt(var + RW_GN_EPS)).reshape(B, T, BRANCH_W) * ln_w + ln_b
    bonus = (jnp.sum(rh * kh * r_k, -1, keepdims=True) * vh).reshape(B, T, BRANCH_W)
    out = (yn + bonus) * g
    return out, S_T, p[:, -1]


def _pool(p, buf, pos0, w_grp, scale):
    B, T, _ = p.shape
    xc = jnp.concatenate([buf, p], axis=1)
    cs = jnp.cumsum(xc.astype(jnp.float32), axis=1)
    cs = jnp.pad(cs, ((0, 0), (1, 0), (0, 0)))
    end = cs[:, POOL_BUF + 1: POOL_BUF + 1 + T]
    t = jnp.arange(T)
    outs = []
    for gi, w in enumerate(POOL_WINDOWS):
        lo, hi = gi * POOL_GW, (gi + 1) * POOL_GW
        start = cs[:, POOL_BUF + 1 - w: POOL_BUF + 1 - w + T, lo:hi]
        cnt = jnp.minimum(w, pos0 + t + 1).astype(jnp.float32)[None, :, None]
        outs.append((end[..., lo:hi] - start) / cnt - p[..., lo:hi].astype(jnp.float32))
    d = jnp.stack(outs, axis=2)
    y = jnp.einsum('btgc,gcd->btgd', d.astype(p.dtype), w_grp).reshape(B, T, BRANCH_W) * scale
    return y, xc[:, -POOL_BUF:]


def _compress(kv, pe, w1, w2):
    B, N = kv.shape[:2]
    nc = (N - CMP_BLOCK) // CMP_STRIDE + 1
    idx = jnp.arange(nc)[:, None] * CMP_STRIDE + jnp.arange(CMP_BLOCK)[None, :]
    blocks = kv[:, idx] + pe[None, None, :, None, :]
    blocks = blocks.transpose(0, 1, 3, 2, 4).reshape(B, nc, NSA_G, CMP_BLOCK * NSA_HD)
    return jax.nn.gelu(blocks @ w1) @ w2


def _overlap(nc, ns):
    cs = jnp.arange(nc)[:, None] * CMP_STRIDE
    ss = jnp.arange(ns)[None, :] * SEL_BLOCK
    return ((cs < ss + SEL_BLOCK) & (cs + CMP_BLOCK > ss)).astype(jnp.float32)


def _cmp_attention(qg, kc, vc, pos0):
    T = qg.shape[1]
    nc = kc.shape[1]
    s = jnp.einsum('btgrd,bcgd->bgrtc', qg, kc).astype(jnp.float32) * NSA_SCALE
    qpos = pos0 + jnp.arange(T)
    bend = jnp.arange(nc) * CMP_STRIDE + CMP_BLOCK - 1
    mask = bend[None, :] <= qpos[:, None]
    pr = jax.nn.softmax(jnp.where(mask, s, NEG), axis=-1) * mask
    o = jnp.einsum('bgrtc,bcgd->btgrd', pr.astype(vc.dtype), vc)
    return o, pr


def _win_attention(qg, kv_new, kv_buf, pos0):
    B, T = qg.shape[:2]
    wb = kv_buf.shape[1]
    kv_cat = jnp.concatenate([kv_buf, kv_new], axis=1)
    Qb = min(Q_BLOCK, T)
    nqb = -(-T // Qb)
    Tp = nqb * Qb
    kv_pad = jnp.pad(kv_cat, ((0, 0), (0, Tp - T), (0, 0), (0, 0), (0, 0)))
    q_pad = jnp.pad(qg, ((0, 0), (0, Tp - T), (0, 0), (0, 0), (0, 0)))

    def block(c):
        start = c * Qb
        qs = lax.dynamic_slice_in_dim(q_pad, start, Qb, axis=1)
        kvs = lax.dynamic_slice_in_dim(kv_pad, start, wb + Qb, axis=1)
        qpos = pos0 + start + jnp.arange(Qb)
        kpos = pos0 - wb + start + jnp.arange(wb + Qb)
        rel = qpos[:, None] - kpos[None, :]
        mask = (rel >= 0) & (rel < WINDOW) & (kpos[None, :] >= 0)
        s = jnp.einsum('bqgrd,bkgd->bgrqk', qs, kvs[:, :, 0]).astype(jnp.float32) * NSA_SCALE
        pr = jax.nn.softmax(jnp.where(mask, s, NEG), axis=-1)
        return jnp.einsum('bgrqk,bkgd->bqgrd', pr.astype(kvs.dtype), kvs[:, :, 1])

    o = lax.map(block, jnp.arange(nqb))
    o = jnp.moveaxis(o, 0, 1).reshape(B, Tp, NSA_G, NSA_R, NSA_HD)[:, :T]
    return o, kv_cat


def _sel_attention_paged(qg, cache_l, page_table, kv_new, imp, pos0):
    B, T = qg.shape[:2]
    page = cache_l.shape[1]
    past = page_table.shape[1] * page
    ns = imp.shape[-1]
    n_sel = min(N_SEL, ns)
    n_past_blk = past // SEL_BLOCK
    bpp = page // SEL_BLOCK
    assert page % SEL_BLOCK == 0 and T <= SEL_BLOCK and ns == n_past_blk + 1 and pos0 == past
    qpos = pos0 + jnp.arange(T)
    cur = qpos // SEL_BLOCK
    blk = jnp.arange(ns)
    sc = jnp.where(blk[None, :] <= cur[:, None], imp, -jnp.inf)
    sc = jnp.where(blk[None, :] == cur[:, None], jnp.inf, sc)
    _, idx = lax.top_k(sc, n_sel)
    idc = jnp.minimum(idx, n_past_blk - 1)
    pp = page_table[jnp.arange(B)[:, None, None, None], idc // bpp]
    hh = idc % bpp
    gi = jnp.arange(NSA_G)[None, :, None, None]
    cache_r = cache_l.reshape(cache_l.shape[0], bpp, SEL_BLOCK, 4, NSA_G, NSA_HD)
    kg = cache_r[pp, hh, :, 2, gi, :]
    vg = cache_r[pp, hh, :, 3, gi, :]
    padn = ((0, 0), (0, SEL_BLOCK - T), (0, 0), (0, 0))
    knew = jnp.pad(kv_new[:, :, 2], padn).transpose(0, 2, 1, 3)[:, :, None, None]
    vnew = jnp.pad(kv_new[:, :, 3], padn).transpose(0, 2, 1, 3)[:, :, None, None]
    is_new = (idx >= n_past_blk)[..., None, None]
    kg = jnp.where(is_new, knew, kg)
    vg = jnp.where(is_new, vnew, vg)
    s = jnp.einsum('bqgrd,bgqnld->bgrqnl', qg, kg).astype(jnp.float32) * NSA_SCALE
    kpos = idx[..., None] * SEL_BLOCK + jnp.arange(SEL_BLOCK)
    valid = kpos <= qpos[None, None, :, None, None]
    s = jnp.where(valid[:, :, None], s, NEG)
    pr = jax.nn.softmax(s.reshape(B, NSA_G, NSA_R, T, n_sel * SEL_BLOCK), axis=-1).reshape(s.shape)
    return jnp.einsum('bgrqnl,bgqnld->bqgrd', pr.astype(vg.dtype), vg)


def _nsa_paged(p, cache_l, page_table, win_buf, pos0, pe, w1, w2):
    B, T, _ = p.shape
    past = page_table.shape[1] * cache_l.shape[1]
    q, kv6, gl = _split(p, (BRANCH_W, NSA_KV_COLS, 3 * NSA_H))
    qg = q.reshape(B, T, NSA_G, NSA_R, NSA_HD)
    kv6 = kv6.reshape(B, T, 6, NSA_G, NSA_HD)
    kv_new = kv6[:, :, :4]
    nc = (past + T - CMP_BLOCK) // CMP_STRIDE + 1
    assert nc == (past - CMP_BLOCK) // CMP_STRIDE + 1
    ck = cache_l[page_table, :, 0:2].reshape(B, past, 2, NSA_G, NSA_HD)
    kc = _compress(ck[:, :, 0], pe[0], w1[0], w2[0])
    vc = _compress(ck[:, :, 1], pe[1], w1[1], w2[1])
    o_cmp, p_cmp = _cmp_attention(qg, kc, vc, pos0)
    ns = -(-(past + T) // SEL_BLOCK)
    imp = jnp.einsum('bgrtc,cs->bgts', p_cmp, _overlap(nc, ns))
    o_sel = _sel_attention_paged(qg, cache_l, page_table, kv_new, imp, pos0)
    o_win, win_cat = _win_attention(qg, kv6[:, :, 4:], win_buf, pos0)
    g = jax.nn.sigmoid(gl.reshape(B, T, 3, NSA_G, NSA_R))[..., None]
    o = g[:, :, 0] * o_cmp + g[:, :, 1] * o_sel + g[:, :, 2] * o_win
    return o.reshape(B, T, BRANCH_W), kv_new, win_cat


def _nsa_nopast(p, win_buf, pe, w1, w2, cdt=jnp.bfloat16, out_dtype=jnp.bfloat16):
    B, T, _ = p.shape
    q, kv6, gl = _split(p, (BRANCH_W, NSA_KV_COLS, 3 * NSA_H))
    kv6 = kv6.reshape(B, T, 6, NSA_G, NSA_HD)
    kv_new = kv6[:, :, :4]
    kc = _compress(kv6[:, :, 0], pe[0], w1[0], w2[0])
    vc = _compress(kv6[:, :, 1], pe[1], w1[1], w2[1])
    o = _nsa_prompt(q, kc, vc, kv6, gl, win_buf, cdt, out_dtype)
    win_cat = jnp.concatenate([win_buf, kv6[:, :, 4:]], axis=1)
    return o, kv_new, win_cat


def _prep_weights(w_in_t, ffn_down):
    return _gate_weights(w_in_t), ffn_down.astype(jnp.bfloat16)


def _layer(l, xpair, pos0, kv_past, win_buf, rw_S, rw_shift, pool_buf, hg_S, ffn_buf, wts, prm, lb_all, win_keep):
    (w_in_t, w_gate_b, w_branch, w_out, ffn_up, ffn_down_b) = wts
    x, x2b = xpair
    B, T, _ = x.shape
    m = B * T
    bf = jnp.bfloat16
    x2 = x.reshape(m, D_MODEL)

    def in_proj(col0, n):
        return _matmul_nt(x2b, w_in_t, l, 2048, IN_TN, col0, n)

    p_rw = in_proj(RW_OFF, RW_COLS).reshape(B, T, RW_COLS)
    p_pool = in_proj(POOL_OFF, BRANCH_W).reshape(B, T, BRANCH_W)
    p_hg = in_proj(HG_OFF, HG_COLS).reshape(B, T, HG_COLS)
    p_nsa = in_proj(NSA_OFF, NSA_COLS_PAD)[:, :NSA_COLS].reshape(B, T, NSA_COLS)
    h_rw, rw_S_new, rw_shift_new = _rwkv7(p_rw, rw_shift, rw_S, prm['rw_mu'][l], prm['rw_w0'][l], prm['rw_w2'][l],
                                          prm['rw_a0'][l], prm['rw_a2'][l], prm['rw_g2'][l], prm['rw_k_k'][l],
                                          prm['rw_k_a'][l], prm['rw_r_k'][l], prm['rw_ln_w'][l], prm['rw_ln_b'][l])
    h_pool, pool_new = _pool(p_pool, pool_buf, pos0, prm['pool_w'][l], prm['pool_scale'][l])
    h_hg, hg_S_new = _hgrn2_pallas(p_hg, hg_S.astype(jnp.float32), lb_all[l], prm['hg_norm_w'][l])
    if kv_past is None:
        h_nsa, kv_new, win_cat = _nsa_nopast(p_nsa, win_buf, prm['cmp_pe'][l], prm['cmp_w1'][l], prm['cmp_w2'][l])
    else:
        h_nsa, kv_new, win_cat = _nsa_paged(p_nsa, kv_past[0], kv_past[1], win_buf, pos0, prm['cmp_pe'][l],
                                            prm['cmp_w1'][l], prm['cmp_w2'][l])
    win_new = win_cat[:, -win_keep:]
    h = jnp.stack([t.reshape(m, BRANCH_W).astype(bf) for t in (h_rw, h_pool, h_hg, h_nsa)])
    merged = _gated_merge(x2b, w_gate_b, h, w_branch, l, 1024, 512)
    y = _matmul(merged.astype(bf), w_out, l, 1024, 512)
    x2, x2b = _residual_layer_norm(x2, y, prm['ln_w'][l, 0], prm['ln_b'][l, 0])
    up = _matmul(x2b, ffn_up, l, 1024, 512)
    hcv = _ffn_activation(up, ffn_buf, prm['ffn_conv'][l], T, 128)
    f = _matmul_ksplit(hcv, ffn_down_b, l, 1024, 512, D_FF // 2)
    x2, x2b = _residual_layer_norm(x2, f, prm['ln_w'][l, 1], prm['ln_b'][l, 1])
    assert T >= CONV_W - 1
    ffn_new = up.reshape(B, T, 2 * D_FF)[:, T - (CONV_W - 1):, :D_FF]
    return (x2.reshape(B, T, D_MODEL), x2b), (kv_new, win_new, rw_S_new, rw_shift_new, pool_new, hg_S_new, ffn_new)


def kernel(x_prompt, x_sample, cache_kv, page_table, state_win, state_rwkv, state_rwkv_shift, state_pool, state_hgrn, state_ffn_conv, w_in, rw_mu, rw_w0, rw_w2, rw_a0, rw_a2, rw_g2, rw_k_k, rw_k_a, rw_r_k, rw_ln_w, rw_ln_b, pool_w, pool_scale, hg_lb_raw, hg_norm_w, cmp_pe, cmp_w1, cmp_w2, w_branch, w_out, ln_w, ln_b, ffn_up, ffn_conv, ffn_down):
    prm = dict(rw_mu=rw_mu, rw_w0=rw_w0, rw_w2=rw_w2, rw_a0=rw_a0, rw_a2=rw_a2, rw_g2=rw_g2, rw_k_k=rw_k_k,
               rw_k_a=rw_k_a, rw_r_k=rw_r_k, rw_ln_w=rw_ln_w, rw_ln_b=rw_ln_b, pool_w=pool_w,
               pool_scale=pool_scale, hg_norm_w=hg_norm_w, cmp_pe=cmp_pe, cmp_w1=cmp_w1, cmp_w2=cmp_w2,
               ln_w=ln_w, ln_b=ln_b, ffn_conv=ffn_conv)
    lb_cum = jnp.cumsum(jax.nn.softmax(hg_lb_raw.astype(jnp.float32), axis=0), axis=0)
    lb_all = lb_cum - lb_cum[:1]
    past_len = page_table.shape[1] * cache_kv.shape[2]
    win_keep = state_win.shape[2]
    w_in_t = jnp.swapaxes(w_in, 1, 2)
    w_gate_b, ffn_down_b = _prep_weights(w_in_t, ffn_down)
    wts = (w_in_t, w_gate_b, w_branch, w_out, ffn_up, ffn_down_b)

    bp = x_prompt.shape[0]
    dt = x_prompt.dtype
    z_win = jnp.zeros((bp, WINDOW, 2, NSA_G, NSA_HD), dt)
    z_rw = jnp.zeros((bp, RW_H, RW_HD, RW_HD), dt)
    z_shift = jnp.zeros((bp, RW_COLS), dt)
    z_pool = jnp.zeros((bp, POOL_BUF, BRANCH_W), dt)
    z_hg = jnp.zeros((bp, HG_H, HG_DK, HG_DV), dt)
    z_ffn = jnp.zeros((bp, CONV_W - 1, D_FF), dt)
    xp = (x_prompt, x_prompt.reshape(-1, D_MODEL).astype(jnp.bfloat16))
    new_p = []
    for l in range(DEPTH):
        xp, st = _layer(l, xp, 0, None, z_win, z_rw, z_shift, z_pool, z_hg, z_ffn, wts, prm, lb_all, win_keep)
        new_p.append(st)

    db = x_sample.shape[0]
    xs = (x_sample, x_sample.reshape(-1, D_MODEL).astype(jnp.bfloat16))
    new_s = []
    for l in range(DEPTH):
        kv_past = (cache_kv[l], page_table)
        xs, st = _layer(l, xs, past_len, kv_past, state_win[l], state_rwkv[l], state_rwkv_shift[l],
                        state_pool[l], state_hgrn[l], state_ffn_conv[l], wts, prm, lb_all, win_keep)
        new_s.append(st)

    sp = [jnp.stack([st[i] for st in new_p]) for i in range(7)]
    ss = [jnp.stack([st[i] for st in new_s]) for i in range(7)]
    return (xp[0], xs[0], sp[0], ss[0], sp[1], ss[1], sp[2], ss[2], sp[3], ss[3], sp[4], ss[4], sp[5], ss[5], sp[6], ss[6])
```

```python
import functools
import math

import jax
import jax.numpy as jnp
import numpy as np
from jax import lax
from jax.experimental import pallas as pl
from jax.experimental.pallas import tpu as pltpu

D_MODEL = 4096
DEPTH = 4
N_BRANCH = 4
BRANCH_W = D_MODEL // N_BRANCH
RW_HD = 64
RW_H = BRANCH_W // RW_HD
RW_W_RANK = 64
RW_A_RANK = 64
RW_G_RANK = 128
RW_COLS = 3 * BRANCH_W + RW_W_RANK + RW_A_RANK + RW_G_RANK
RW_GN_EPS = 64e-5
POOL_WINDOWS = (2, 4, 8, 16)
POOL_GW = BRANCH_W // len(POOL_WINDOWS)
POOL_BUF = max(POOL_WINDOWS) - 1
HG_DK = 128
HG_H = BRANCH_W // HG_DK
HG_DV = BRANCH_W // HG_H
HG_FDIM = HG_H * HG_DK
HG_COLS = 2 * HG_FDIM + 2 * BRANCH_W
HG_CHUNK = 64
HG_EPS = 1e-6
NSA_HD = 64
NSA_H = BRANCH_W // NSA_HD
NSA_G = 4
NSA_R = NSA_H // NSA_G
NSA_SCALE = NSA_HD ** -0.5
CMP_BLOCK = 32
CMP_STRIDE = 16
CMP_HIDDEN = 128
SEL_BLOCK = 64
N_SEL = 8
WINDOW = 512
Q_BLOCK = 128
NSA_KV_COLS = 6 * NSA_G * NSA_HD
NSA_COLS = BRANCH_W + NSA_KV_COLS + 3 * NSA_H
GATE_COLS = N_BRANCH * D_MODEL
IN_COLS = RW_COLS + BRANCH_W + HG_COLS + NSA_COLS + GATE_COLS
D_FF = 256 * ((8 * D_MODEL // 3 + 255) // 256)
CONV_W = 3
ALPHA = (2 * DEPTH) ** 0.25
LN_EPS = 1e-5
NEG = -1e30

RW_OFF = 0
POOL_OFF = RW_OFF + RW_COLS
HG_OFF = POOL_OFF + BRANCH_W
NSA_OFF = HG_OFF + HG_COLS
MIX_COLS = NSA_OFF + NSA_COLS
IN_TN = 256
NSA_COLS_PAD = -(-NSA_COLS // IN_TN) * IN_TN

VMEM_LIMIT = 56 * 1024 * 1024
LANES = 128
SUBLANES = 8


GATE_BLK0 = MIX_COLS // LANES
GATE_SHIFT = MIX_COLS % LANES
GATE_TN = 512
assert GATE_SHIFT % (2 * SUBLANES) == 0


def _gate_repack_kernel(*refs):
    o_ref = refs[-1]
    x = jnp.concatenate([r[...] for r in refs[:-1]], axis=0)
    o_ref[...] = x[GATE_SHIFT:GATE_SHIFT + GATE_TN, :].astype(o_ref.dtype)


def _gate_weights(w_in_t):
    nl, _, d = w_in_t.shape
    nin = GATE_TN // LANES + 1
    specs = [pl.BlockSpec((None, LANES, d),
                          lambda l, j, k=k: (l, GATE_BLK0 + (GATE_TN // LANES) * j + k, 0)) for k in range(nin)]
    return pl.pallas_call(
        _gate_repack_kernel,
        grid=(nl, GATE_COLS // GATE_TN),
        in_specs=specs,
        out_specs=pl.BlockSpec((None, GATE_TN, d), lambda l, j: (l, j, 0)),
        out_shape=jax.ShapeDtypeStruct((nl, GATE_COLS, d), jnp.bfloat16),
        compiler_params=pltpu.CompilerParams(
            dimension_semantics=("arbitrary", "arbitrary"), vmem_limit_bytes=VMEM_LIMIT),
        name="gate_repack",
    )(*([w_in_t] * nin))


NT_DIMS = (((1,), (1,)), ((), ()))


def _mm_kernel(a_ref, b_ref, o_ref):
    b = b_ref[...].astype(a_ref.dtype)
    o_ref[...] = jnp.dot(a_ref[...], b, preferred_element_type=jnp.float32).astype(o_ref.dtype)


def _mm_nt_kernel(a_ref, b_ref, o_ref):
    b = b_ref[...].astype(a_ref.dtype)
    o_ref[...] = lax.dot_general(a_ref[...], b, NT_DIMS, preferred_element_type=jnp.float32).astype(o_ref.dtype)


def _matmul(a, w, l, tm, tn, out_dtype=jnp.float32):
    m, k = a.shape
    n = w.shape[2]
    tm = min(tm, m)
    assert m % tm == 0 and n % tn == 0, (m, n, tm, tn)
    return pl.pallas_call(
        _mm_kernel,
        grid=(m // tm, n // tn),
        in_specs=[pl.BlockSpec((tm, k), lambda i, j: (i, 0)),
                  pl.BlockSpec((None, k, tn), lambda i, j: (l, 0, j))],
        out_specs=pl.BlockSpec((tm, tn), lambda i, j: (i, j)),
        out_shape=jax.ShapeDtypeStruct((m, n), out_dtype),
        compiler_params=pltpu.CompilerParams(
            dimension_semantics=("arbitrary", "arbitrary"), vmem_limit_bytes=VMEM_LIMIT),
        name="matmul",
    )(a, w)


def _matmul_nt(a, wt, l, tm, tn, row0, n, out_dtype=jnp.float32):
    m, k = a.shape
    tm = min(tm, m)
    assert m % tm == 0 and n % tn == 0 and row0 % tn == 0 and row0 + n <= wt.shape[1], (m, n, tm, tn, row0)
    r0 = row0 // tn
    return pl.pallas_call(
        _mm_nt_kernel,
        grid=(m // tm, n // tn),
        in_specs=[pl.BlockSpec((tm, k), lambda i, j: (i, 0)),
                  pl.BlockSpec((None, tn, k), lambda i, j: (l, r0 + j, 0))],
        out_specs=pl.BlockSpec((tm, tn), lambda i, j: (i, j)),
        out_shape=jax.ShapeDtypeStruct((m, n), out_dtype),
        compiler_params=pltpu.CompilerParams(
            dimension_semantics=("arbitrary", "arbitrary"), vmem_limit_bytes=VMEM_LIMIT),
        name="matmul_nt",
    )(a, wt)


def _mm_ksplit_kernel(a_ref, b_ref, o_ref):
    @pl.when(pl.program_id(2) == 0)
    def _():
        o_ref[...] = jnp.zeros_like(o_ref)

    o_ref[...] += jnp.dot(a_ref[...], b_ref[...], preferred_element_type=jnp.float32)


def _matmul_ksplit(a, w, l, tm, tn, tk):
    m, k = a.shape
    n = w.shape[2]
    tm = min(tm, m)
    assert m % tm == 0 and n % tn == 0 and k % tk == 0
    return pl.pallas_call(
        _mm_ksplit_kernel,
        grid=(m // tm, n // tn, k // tk),
        in_specs=[pl.BlockSpec((tm, tk), lambda i, j, kk: (i, kk)),
                  pl.BlockSpec((None, tk, tn), lambda i, j, kk: (l, kk, j))],
        out_specs=pl.BlockSpec((tm, tn), lambda i, j, kk: (i, j)),
        out_shape=jax.ShapeDtypeStruct((m, n), jnp.float32),
        compiler_params=pltpu.CompilerParams(
            dimension_semantics=("arbitrary", "arbitrary", "arbitrary"), vmem_limit_bytes=VMEM_LIMIT),
        name="matmul_ksplit",
    )(a, w)


def _merge_kernel(x_ref, wg_ref, h_ref, wb_ref, o_ref):
    n = pl.program_id(2)

    @pl.when(n == 0)
    def _():
        o_ref[...] = jnp.zeros_like(o_ref)

    gate = jax.nn.sigmoid(lax.dot_general(x_ref[...], wg_ref[...], NT_DIMS, preferred_element_type=jnp.float32))
    y = jnp.dot(h_ref[0], wb_ref[0].astype(h_ref.dtype), preferred_element_type=jnp.float32)
    o_ref[...] += gate * y


def _gated_merge(x, w_gate, h, w_branch, l, tm, tn):
    nb, m, bw = h.shape
    d = w_branch.shape[-1]
    tm = min(tm, m)
    nj = d // tn
    assert m % tm == 0 and d % tn == 0
    return pl.pallas_call(
        _merge_kernel,
        grid=(m // tm, nj, nb),
        in_specs=[pl.BlockSpec((tm, d), lambda i, j, n: (i, 0)),
                  pl.BlockSpec((None, tn, d), lambda i, j, n: (l, n * nj + j, 0)),
                  pl.BlockSpec((1, tm, bw), lambda i, j, n: (n, i, 0)),
                  pl.BlockSpec((None, 1, bw, tn), lambda i, j, n: (l, n, 0, j))],
        out_specs=pl.BlockSpec((tm, tn), lambda i, j, n: (i, j)),
        out_shape=jax.ShapeDtypeStruct((m, d), jnp.float32),
        compiler_params=pltpu.CompilerParams(
            dimension_semantics=("arbitrary", "arbitrary", "arbitrary"), vmem_limit_bytes=VMEM_LIMIT),
        name="gated_merge",
    )(x, w_gate, h, w_branch)


LN_ROWS = 256


def _ln_kernel(x_ref, y_ref, w_ref, b_ref, o_ref, ob_ref):
    z = ALPHA * x_ref[...] + y_ref[...]
    mu = jnp.mean(z, axis=-1, keepdims=True)
    zc = z - mu
    var = jnp.mean(zc * zc, axis=-1, keepdims=True)
    o = zc * lax.rsqrt(var + LN_EPS) * w_ref[...] + b_ref[...]
    o_ref[...] = o
    ob_ref[...] = o.astype(ob_ref.dtype)


def _residual_layer_norm(x, y, w, b):
    m, d = x.shape
    tr = min(LN_ROWS, m)
    assert m % tr == 0
    row = pl.BlockSpec((tr, d), lambda i: (i, 0))
    vec = pl.BlockSpec((1, d), lambda i: (0, 0))
    return pl.pallas_call(
        _ln_kernel,
        grid=(m // tr,),
        in_specs=[row, row, vec, vec],
        out_specs=[row, row],
        out_shape=[jax.ShapeDtypeStruct((m, d), jnp.float32), jax.ShapeDtypeStruct((m, d), jnp.bfloat16)],
        compiler_params=pltpu.CompilerParams(dimension_semantics=("arbitrary",), vmem_limit_bytes=VMEM_LIMIT),
        name="residual_layer_norm",
    )(x, y, w.reshape(1, d), b.reshape(1, d))


FFN_ACT_COLS = D_FF // 2


def _ffn_act_kernel(u_ref, halo_ref, buf_ref, v_ref, wc_ref, o_ref, *, blocks_per_seq):
    i = pl.program_id(1)
    u = u_ref[...]
    first = (i % blocks_per_seq) == 0
    halo = halo_ref[...]
    prev = jnp.where(first, buf_ref[0], halo[SUBLANES - (CONV_W - 1):, :])
    ridx = lax.broadcasted_iota(jnp.int32, u.shape, 0)
    c = wc_ref[CONV_W - 1:CONV_W, :] * u
    for back in range(1, CONV_W):
        sh = pltpu.roll(u, back, axis=0)
        for j in range(back):
            sh = jnp.where(ridx == j, prev[CONV_W - 1 - back + j:CONV_W - back + j, :], sh)
        c = c + wc_ref[CONV_W - 1 - back:CONV_W - back, :] * sh
    o_ref[...] = (jax.nn.gelu(c) * v_ref[...]).astype(o_ref.dtype)


def _ffn_activation(up, buf, w_conv, seq_len, tr):
    m = up.shape[0]
    tr = min(tr, seq_len)
    tc = FFN_ACT_COLS
    ncb = D_FF // tc
    assert seq_len % tr == 0 and tr % SUBLANES == 0 and m % seq_len == 0
    bps = seq_len // tr
    hb = tr // SUBLANES
    return pl.pallas_call(
        functools.partial(_ffn_act_kernel, blocks_per_seq=bps),
        grid=(ncb, m // tr),
        in_specs=[pl.BlockSpec((tr, tc), lambda j, i: (i, j)),
                  pl.BlockSpec((SUBLANES, tc), lambda j, i: (jnp.maximum(i * hb - 1, 0), j)),
                  pl.BlockSpec((1, CONV_W - 1, tc), lambda j, i: (i // bps, 0, j)),
                  pl.BlockSpec((tr, tc), lambda j, i: (i, ncb + j)),
                  pl.BlockSpec((CONV_W, tc), lambda j, i: (0, j))],
        out_specs=pl.BlockSpec((tr, tc), lambda j, i: (i, j)),
        out_shape=jax.ShapeDtypeStruct((m, D_FF), jnp.bfloat16),
        compiler_params=pltpu.CompilerParams(
            dimension_semantics=("arbitrary", "arbitrary"), vmem_limit_bytes=VMEM_LIMIT),
        name="ffn_activation",
    )(up, up, buf, up, w_conv)


RW_TB = 64
RW_NK = 5


def _rw_rec_kernel(kx_ref, v_ref, s0_ref, y_ref, st_ref, s_scr, kx_scr, *, tb, vs, fold):
    i = pl.program_id(0)

    @pl.when(i == 0)
    def _():
        s_scr[...] = s0_ref[...]

    def widen(t, carry):
        x = kx_ref[t]
        kx_scr[t] = x if fold == 1 else jnp.concatenate([x] * fold, axis=-1)
        return carry

    lax.fori_loop(0, tb, widen, 0, unroll=2)

    def step(t, carry):
        def group(gi, c2):
            base = pl.multiple_of(gi * SUBLANES, SUBLANES)
            vrows = v_ref[t, pl.ds(base, SUBLANES), :]
            ys = []
            for j in range(SUBLANES):
                s = s_scr[base + j]
                sa = jnp.sum(s * kx_scr[t, 0], axis=0, keepdims=True)
                s = s * kx_scr[t, 2] + sa * kx_scr[t, 1] + vrows[j:j + 1, :] * kx_scr[t, 3]
                s_scr[base + j] = s
                ys.append(jnp.sum(s * kx_scr[t, 4], axis=0, keepdims=True))
            y_ref[t, pl.ds(base, SUBLANES), :] = jnp.concatenate(ys, axis=0)
            return c2

        lax.fori_loop(0, vs // SUBLANES, group, 0)
        return carry

    lax.fori_loop(0, tb, step, 0)

    @pl.when(i == pl.num_programs(0) - 1)
    def _():
        st_ref[...] = s_scr[...]


def _rwkv_recurrence(rh, k2h, vh, kkh, ah, dh, s0):
    b, t, h, hd = rh.shape
    fold = LANES // (b * h)
    vs = hd // fold
    tb = min(RW_TB, t)
    assert fold * b * h == LANES and t % tb == 0 and vs % SUBLANES == 0
    kx = jnp.stack([-kkh, kkh * ah, dh, k2h, rh])
    kx = kx.transpose(2, 0, 4, 1, 3).reshape(t, RW_NK, hd, b * h)
    vv = vh.reshape(b, t, h, fold, vs).transpose(1, 4, 3, 0, 2).reshape(t, vs, LANES)
    s0l = s0.reshape(b, h, fold, vs, hd).transpose(3, 4, 2, 0, 1).reshape(vs, hd, LANES)
    y, st = pl.pallas_call(
        functools.partial(_rw_rec_kernel, tb=tb, vs=vs, fold=fold),
        grid=(t // tb,),
        in_specs=[pl.BlockSpec((tb, RW_NK, hd, b * h), lambda i: (i, 0, 0, 0)),
                  pl.BlockSpec((tb, vs, LANES), lambda i: (i, 0, 0)),
                  pl.BlockSpec((vs, hd, LANES), lambda i: (0, 0, 0))],
        out_specs=[pl.BlockSpec((tb, vs, LANES), lambda i: (i, 0, 0)),
                   pl.BlockSpec((vs, hd, LANES), lambda i: (0, 0, 0))],
        out_shape=[jax.ShapeDtypeStruct((t, vs, LANES), jnp.float32),
                   jax.ShapeDtypeStruct((vs, hd, LANES), jnp.float32)],
        scratch_shapes=[pltpu.VMEM((vs, hd, LANES), jnp.float32), pltpu.VMEM((tb, RW_NK, hd, LANES), jnp.float32)],
        compiler_params=pltpu.CompilerParams(
            dimension_semantics=("arbitrary",), vmem_limit_bytes=VMEM_LIMIT),
        name="rwkv_recurrence",
    )(kx, vv, s0l)
    y = y.reshape(t, vs, fold, b, h).transpose(3, 0, 4, 2, 1).reshape(b, t, h, hd)
    st = st.reshape(vs, hd, fold, b, h).transpose(3, 4, 2, 0, 1).reshape(b, h, hd, hd)
    return y, st


HG_CHUNK_ROWS = 128
HG_SUB = 16
HG_BLOCK_ROWS = 256


def _hgrn_kernel(q_ref, f_ref, i_ref, g_ref, lb_ref, nw_ref, s0_ref, o_ref, st_ref, s_scr, *, n_valid, cdt):
    f32 = jnp.float32
    ch, sub = HG_CHUNK_ROWS, HG_SUB
    tb = pl.program_id(2)

    @pl.when(tb == 0)
    def _():
        s_scr[...] = s0_ref[0, 0]

    lb = lb_ref[...]
    log_lb = jnp.log(lb)
    log_1mlb = jnp.log1p(-lb)
    rows = q_ref.shape[0]
    tril = (lax.broadcasted_iota(jnp.int32, (ch, ch), 0) >= lax.broadcasted_iota(jnp.int32, (ch, ch), 1)).astype(f32)
    rowc = lax.broadcasted_iota(jnp.int32, (ch, HG_DK), 0)
    trow = lax.broadcasted_iota(jnp.int32, (sub, HG_DK), 0)
    lane = lax.broadcasted_iota(jnp.int32, (sub, ch), 1)
    nt = (((1,), (1,)), ((), ()))
    for c in range(rows // ch):
        sl = slice(c * ch, (c + 1) * ch)
        ff = f_ref[sl, :]
        log_sig = jnp.minimum(ff, 0.0) - jnp.log1p(jnp.exp(-jnp.abs(ff)))
        x2 = log_1mlb + log_sig
        log_f = jnp.maximum(log_lb, x2) + jnp.log1p(jnp.exp(-jnp.abs(log_lb - x2)))
        kk = (1.0 - lb) * jax.nn.sigmoid(-ff)
        if n_valid < rows:
            ok = rowc < (n_valid - c * ch)
            log_f = jnp.where(ok, log_f, 0.0)
            kk = jnp.where(ok, kk, 0.0)
        q = q_ref[sl, :]
        qq = q * jax.nn.sigmoid(q)
        vv = i_ref[sl, :]
        b = jnp.dot(tril, log_f, preferred_element_type=f32, precision=lax.Precision.HIGHEST)
        st = s_scr[...]
        inter = lax.dot_general((qq * jnp.exp(b)).astype(cdt), st.astype(cdt), nt, preferred_element_type=f32)
        strips = []
        for i in range(ch // sub):
            r0 = i * sub
            bi = b[r0:r0 + sub]
            qi = qq[r0:r0 + sub]
            ki = kk[r0:r0 + sub]
            if i > 0:
                bprev = b[r0 - 1:r0]
                ks = jnp.where(rowc < r0, kk * jnp.exp(jnp.minimum(bprev - b, 0.0)), 0.0)
                qs = qi * jnp.exp(bi - bprev)
                strip = lax.dot_general(qs.astype(cdt), ks.astype(cdt), nt, preferred_element_type=f32)
            else:
                strip = jnp.zeros((sub, ch), f32)
            for s in range(sub):
                keep = trow >= s
                e = jnp.exp(jnp.where(keep, bi - bi[s:s + 1, :], 0.0))
                col = jnp.sum(jnp.where(keep, qi * ki[s:s + 1, :] * e, 0.0), axis=-1, keepdims=True)
                strip = jnp.where(lane == r0 + s, col, strip)
            strips.append(strip)
        att = jnp.concatenate(strips, axis=0)
        o = inter + jnp.dot(att.astype(cdt), vv.astype(cdt), preferred_element_type=f32)
        bl = b[ch - 1:ch, :]
        kd = kk * jnp.exp(bl - b)
        s_scr[...] = st * jnp.exp(bl) + jnp.dot(vv.T.astype(cdt), kd.astype(cdt), preferred_element_type=f32)
        o = o * lax.rsqrt(jnp.mean(o * o, axis=-1, keepdims=True) + HG_EPS)
        g = g_ref[sl, :]
        o_ref[sl, :] = (o * nw_ref[...] * (g * jax.nn.sigmoid(g))).astype(o_ref.dtype)

    @pl.when(tb == pl.num_programs(2) - 1)
    def _():
        st_ref[0, 0] = s_scr[...]


def _hgrn2_pallas(p, s0, lb, norm_w, cdt=jnp.bfloat16, out_dtype=jnp.bfloat16):
    b, t, _ = p.shape
    assert HG_DK == LANES and HG_DV == LANES
    rows = min(HG_BLOCK_ROWS, -(-t // HG_CHUNK_ROWS) * HG_CHUNK_ROWS)
    tp = -(-t // rows) * rows
    n_valid = rows if tp == t else t
    assert tp == t or tp == rows
    if tp != t:
        p = jnp.pad(p, ((0, 0), (0, tp - t), (0, 0)))
    p2 = p.reshape(b * tp, HG_COLS)
    nt = tp // rows
    s0t = jnp.swapaxes(s0, 2, 3)
    col = lambda off: pl.BlockSpec((rows, LANES), lambda bi, hi, ti: (bi * nt + ti, off + hi))
    vec = pl.BlockSpec((1, LANES), lambda bi, hi, ti: (0, hi))
    sspec = pl.BlockSpec((1, 1, HG_DV, HG_DK), lambda bi, hi, ti: (bi, hi, 0, 0))
    o, st = pl.pallas_call(
        functools.partial(_hgrn_kernel, n_valid=n_valid, cdt=cdt),
        grid=(b, HG_H, nt),
        in_specs=[col(0), col(HG_H), col(2 * HG_H), col(3 * HG_H), vec, vec, sspec],
        out_specs=[pl.BlockSpec((rows, LANES), lambda bi, hi, ti: (bi * nt + ti, hi)), sspec],
        out_shape=[jax.ShapeDtypeStruct((b * tp, BRANCH_W), out_dtype),
                   jax.ShapeDtypeStruct((b, HG_H, HG_DV, HG_DK), jnp.float32)],
        scratch_shapes=[pltpu.VMEM((HG_DV, HG_DK), jnp.float32)],
        compiler_params=pltpu.CompilerParams(
            dimension_semantics=("arbitrary", "arbitrary", "arbitrary"), vmem_limit_bytes=VMEM_LIMIT),
        name="hgrn2",
    )(p2, p2, p2, p2, lb.reshape(1, HG_FDIM), norm_w.reshape(1, BRANCH_W), s0t)
    if tp != t:
        o = o.reshape(b, tp, BRANCH_W)[:, :t].reshape(b * t, BRANCH_W)
    return o, jnp.swapaxes(st, 2, 3)


NSA_KEY_STEP = 256


def _nsa_kernel(q_ref, kc_ref, vc_ref, ks_ref, vs_ref, kw_ref, vw_ref, gl_ref, ov_ref, e_ref, o_ref,
                *, n_keys, wb):
    f32 = jnp.float32
    cdt = q_ref.dtype
    qb = Q_BLOCK
    qi = pl.program_id(2)
    nt = (((1,), (1,)), ((), ()))
    tq = lax.broadcasted_iota(jnp.int32, (qb, 1), 0) + qi * qb

    def softmax_parts(s, mask):
        s = jnp.where(mask, s, NEG)
        e = jnp.exp(s - jnp.max(s, axis=-1, keepdims=True))
        return e, jnp.sum(e, axis=-1, keepdims=True)

    cidx = lax.broadcasted_iota(jnp.int32, (qb, LANES), 1)
    maskc = (cidx * CMP_STRIDE + (CMP_BLOCK - 1)) <= tq
    kc = kc_ref[0, 0]
    vc = vc_ref[0, 0]
    psum = jnp.zeros((qb, LANES), f32)
    o_cmp = []
    for r in range(NSA_R):
        q_r = q_ref[0, 0, 0, r * qb:(r + 1) * qb, :]
        s = lax.dot_general(q_r, kc, nt, preferred_element_type=f32) * NSA_SCALE
        e, l = softmax_parts(s, maskc)
        pr = jnp.where(maskc, e / l, 0.0)
        psum = psum + pr
        o_cmp.append(jnp.dot(pr.astype(cdt), vc, preferred_element_type=f32))

    imp = jnp.dot(psum, ov_ref[...], preferred_element_type=f32, precision=lax.Precision.HIGHEST)
    cur = lax.shift_right_logical(tq, int(math.log2(SEL_BLOCK)))
    sc = jnp.where(cidx <= cur, imp, -jnp.inf)
    sc = jnp.where(cidx == cur, jnp.inf, sc)
    ns = n_keys // SEL_BLOCK
    sct = sc.T[:ns]
    sidx = lax.broadcasted_iota(jnp.int32, (ns, qb), 0)
    rank = jnp.zeros((ns, qb), jnp.int32)
    for sp in range(ns):
        row = sct[sp:sp + 1, :]
        beats = (row > sct) | ((row == sct) & (sidx > sp))
        rank = rank + beats.astype(jnp.int32)
    selt = (rank < N_SEL).astype(f32)
    if ns < LANES:
        selt = jnp.concatenate([selt, jnp.zeros((LANES - ns, qb), f32)], axis=0)
    sel = selt.T.astype(jnp.bfloat16)

    kposw = lax.broadcasted_iota(jnp.int32, (qb, wb + qb), 1) + (qi * qb - wb)
    rel = tq - kposw
    maskw = (rel >= 0) & (rel < WINDOW) & (kposw >= 0)
    wstart = pl.multiple_of(qi * qb, qb)
    kw = kw_ref[0, 0, pl.ds(wstart, wb + qb), :]
    vw = vw_ref[0, 0, pl.ds(wstart, wb + qb), :]
    gates, o_cw = [], []
    for r in range(NSA_R):
        q_r = q_ref[0, 0, 0, r * qb:(r + 1) * qb, :]
        s = lax.dot_general(q_r, kw, nt, preferred_element_type=f32) * NSA_SCALE
        e, l = softmax_parts(s, maskw)
        o_win = jnp.dot(e.astype(cdt), vw, preferred_element_type=f32) / l
        g = jax.nn.sigmoid(gl_ref[0, 0, 0, r * qb:(r + 1) * qb, :])
        gates.append(g[:, 1:2])
        o_cw.append(g[:, 0:1] * o_cmp[r] + g[:, 2:3] * o_win)

    def finish(kext):
        msel = jnp.dot(sel, e_ref[:, :kext], preferred_element_type=f32)
        kpos = lax.broadcasted_iota(jnp.int32, (qb, kext), 1)
        valid = (msel > 0.5) & (kpos <= tq)
        ks = ks_ref[0, 0, :kext, :]
        vs = vs_ref[0, 0, :kext, :]
        outs = []
        for r in range(NSA_R):
            q_r = q_ref[0, 0, 0, r * qb:(r + 1) * qb, :]
            s = lax.dot_general(q_r, ks, nt, preferred_element_type=f32) * NSA_SCALE
            e, l = softmax_parts(s, valid)
            o_sel = jnp.dot(e.astype(cdt), vs, preferred_element_type=f32) / l
            outs.append(o_cw[r] + gates[r] * o_sel)
        o_ref[...] = jnp.concatenate(outs, axis=1).astype(o_ref.dtype)

    step = NSA_KEY_STEP if n_keys % NSA_KEY_STEP == 0 else n_keys
    for kext in range(step, n_keys + 1, step):
        @pl.when((qi >= (kext - step) // qb) & (qi < kext // qb))
        def _(kext=kext):
            finish(kext)


def _nsa_prompt(q, kc, vc, kv6, gl, win_buf, cdt, out_dtype):
    b, t, _ = q.shape
    qb = Q_BLOCK
    nqb = t // qb
    wb = win_buf.shape[1]
    nc = kc.shape[1]
    assert t % qb == 0 and nc <= LANES and t // SEL_BLOCK <= LANES and t // SEL_BLOCK >= N_SEL
    qa = q.reshape(b, nqb, qb, NSA_G, NSA_R, NSA_HD).transpose(0, 3, 1, 4, 2, 5)
    qa = qa.reshape(b, NSA_G, nqb, NSA_R * qb, NSA_HD).astype(cdt)
    gla = gl.reshape(b, nqb, qb, 3, NSA_G, NSA_R).transpose(0, 4, 1, 5, 2, 3).reshape(b, NSA_G, nqb, NSA_R * qb, 3)

    def bg(x):
        return x.transpose(0, 2, 1, 3).astype(cdt)

    pad = ((0, 0), (0, LANES - nc), (0, 0), (0, 0))
    kca, vca = bg(jnp.pad(kc, pad)), bg(jnp.pad(vc, pad))
    ksa, vsa = bg(kv6[:, :, 2]), bg(kv6[:, :, 3])
    kwa = bg(jnp.concatenate([win_buf[:, :, 0], kv6[:, :, 4]], axis=1))
    vwa = bg(jnp.concatenate([win_buf[:, :, 1], kv6[:, :, 5]], axis=1))
    ns = t // SEL_BLOCK
    ov = np.zeros((LANES, LANES), np.float32)
    cs = np.arange(nc)[:, None] * CMP_STRIDE
    ss = np.arange(ns)[None, :] * SEL_BLOCK
    ov[:nc, :ns] = (cs < ss + SEL_BLOCK) & (cs + CMP_BLOCK > ss)
    e = (np.arange(LANES)[:, None] == (np.arange(t)[None, :] // SEL_BLOCK)).astype(np.float32)
    kvspec = lambda n: pl.BlockSpec((1, 1, n, NSA_HD), lambda bi, gi, qi: (bi, gi, 0, 0))
    return pl.pallas_call(
        functools.partial(_nsa_kernel, n_keys=t, wb=wb),
        grid=(b, NSA_G, nqb),
        in_specs=[pl.BlockSpec((1, 1, 1, NSA_R * qb, NSA_HD), lambda bi, gi, qi: (bi, gi, qi, 0, 0)),
                  kvspec(LANES), kvspec(LANES), kvspec(t), kvspec(t), kvspec(wb + t), kvspec(wb + t),
                  pl.BlockSpec((1, 1, 1, NSA_R * qb, 3), lambda bi, gi, qi: (bi, gi, qi, 0, 0)),
                  pl.BlockSpec((LANES, LANES), lambda bi, gi, qi: (0, 0)),
                  pl.BlockSpec((LANES, t), lambda bi, gi, qi: (0, 0))],
        out_specs=pl.BlockSpec((qb, NSA_R * NSA_HD), lambda bi, gi, qi: (bi * nqb + qi, gi)),
        out_shape=jax.ShapeDtypeStruct((b * t, BRANCH_W), out_dtype),
        compiler_params=pltpu.CompilerParams(
            dimension_semantics=("arbitrary", "arbitrary", "arbitrary"), vmem_limit_bytes=VMEM_LIMIT),
        name="nsa_prompt",
    )(qa, kca, vca, ksa, vsa, kwa, vwa, gla, jnp.asarray(ov), jnp.asarray(e, jnp.bfloat16))


def _split(x, sizes):
    idx = tuple(int(s) for s in np.cumsum(sizes)[:-1])
    return jnp.split(x, idx, axis=-1)


def _rwkv7(p, shift_prev, S0, mu, w0, w2, a0, a2, g2, k_k, k_a, r_k, ln_w, ln_b):
    B, T, _ = p.shape
    f32 = jnp.float32
    p_prev = jnp.concatenate([shift_prev[:, None], p[:, :-1]], axis=1)
    xs = p + mu * (p_prev - p)
    r, k, v, wl, al, gl = _split(xs, (BRANCH_W, BRANCH_W, BRANCH_W, RW_W_RANK, RW_A_RANK, RW_G_RANK))
    w = -jax.nn.softplus(-(w0 + jnp.tanh(wl) @ w2).astype(f32)) - 0.5
    decay = jnp.exp(-jnp.exp(w))
    a = jax.nn.sigmoid((a0 + al @ a2).astype(f32))
    g = jax.nn.sigmoid(gl) @ g2

    def heads(t):
        return t.astype(f32).reshape(B, T, RW_H, RW_HD)

    kk = heads(k * k_k)
    kk = kk / jnp.maximum(jnp.linalg.norm(kk, axis=-1, keepdims=True), 1e-12)
    k2 = k.astype(f32) * (1.0 + (a - 1.0) * k_a)
    rh, kh, vh, ah, dh = heads(r), heads(k2), heads(v), heads(a), heads(decay)

    y, S_T = _rwkv_recurrence(rh, kh, vh, kk, ah, dh, S0.astype(f32))
    mean = y.mean(-1, keepdims=True)
    var = jnp.mean(jnp.square(y - mean), -1, keepdims=True)
    yn = ((y - mean) * lax.rsqrt(var + RW_GN_EPS)).reshape(B, T, BRANCH_W) * ln_w + ln_b
    bonus = (jnp.sum(rh * kh * r_k, -1, keepdims=True) * vh).reshape(B, T, BRANCH_W)
    out = (yn + bonus) * g
    return out, S_T, p[:, -1]


def _pool(p, buf, pos0, w_grp, scale):
    B, T, _ = p.shape
    xc = jnp.concatenate([buf, p], axis=1)
    cs = jnp.cumsum(xc.astype(jnp.float32), axis=1)
    cs = jnp.pad(cs, ((0, 0), (1, 0), (0, 0)))
    end = cs[:, POOL_BUF + 1: POOL_BUF + 1 + T]
    t = jnp.arange(T)
    outs = []
    for gi, w in enumerate(POOL_WINDOWS):
        lo, hi = gi * POOL_GW, (gi + 1) * POOL_GW
        start = cs[:, POOL_BUF + 1 - w: POOL_BUF + 1 - w + T, lo:hi]
        cnt = jnp.minimum(w, pos0 + t + 1).astype(jnp.float32)[None, :, None]
        outs.append((end[..., lo:hi] - start) / cnt - p[..., lo:hi].astype(jnp.float32))
    d = jnp.stack(outs, axis=2)
    y = jnp.einsum('btgc,gcd->btgd', d.astype(p.dtype), w_grp).reshape(B, T, BRANCH_W) * scale
    return y, xc[:, -POOL_BUF:]


def _compress(kv, pe, w1, w2):
    B, N = kv.shape[:2]
    nc = (N - CMP_BLOCK) // CMP_STRIDE + 1
    idx = jnp.arange(nc)[:, None] * CMP_STRIDE + jnp.arange(CMP_BLOCK)[None, :]
    blocks = kv[:, idx] + pe[None, None, :, None, :]
    blocks = blocks.transpose(0, 1, 3, 2, 4).reshape(B, nc, NSA_G, CMP_BLOCK * NSA_HD)
    return jax.nn.gelu(blocks @ w1) @ w2


def _overlap(nc, ns):
    cs = jnp.arange(nc)[:, None] * CMP_STRIDE
    ss = jnp.arange(ns)[None, :] * SEL_BLOCK
    return ((cs < ss + SEL_BLOCK) & (cs + CMP_BLOCK > ss)).astype(jnp.float32)


def _cmp_attention(qg, kc, vc, pos0):
    T = qg.shape[1]
    nc = kc.shape[1]
    s = jnp.einsum('btgrd,bcgd->bgrtc', qg, kc).astype(jnp.float32) * NSA_SCALE
    qpos = pos0 + jnp.arange(T)
    bend = jnp.arange(nc) * CMP_STRIDE + CMP_BLOCK - 1
    mask = bend[None, :] <= qpos[:, None]
    pr = jax.nn.softmax(jnp.where(mask, s, NEG), axis=-1) * mask
    o = jnp.einsum('bgrtc,bcgd->btgrd', pr.astype(vc.dtype), vc)
    return o, pr


def _win_attention(qg, kv_new, kv_buf, pos0):
    B, T = qg.shape[:2]
    wb = kv_buf.shape[1]
    kv_cat = jnp.concatenate([kv_buf, kv_new], axis=1)
    Qb = min(Q_BLOCK, T)
    nqb = -(-T // Qb)
    Tp = nqb * Qb
    kv_pad = jnp.pad(kv_cat, ((0, 0), (0, Tp - T), (0, 0), (0, 0), (0, 0)))
    q_pad = jnp.pad(qg, ((0, 0), (0, Tp - T), (0, 0), (0, 0), (0, 0)))

    def block(c):
        start = c * Qb
        qs = lax.dynamic_slice_in_dim(q_pad, start, Qb, axis=1)
        kvs = lax.dynamic_slice_in_dim(kv_pad, start, wb + Qb, axis=1)
        qpos = pos0 + start + jnp.arange(Qb)
        kpos = pos0 - wb + start + jnp.arange(wb + Qb)
        rel = qpos[:, None] - kpos[None, :]
        mask = (rel >= 0) & (rel < WINDOW) & (kpos[None, :] >= 0)
        s = jnp.einsum('bqgrd,bkgd->bgrqk', qs, kvs[:, :, 0]).astype(jnp.float32) * NSA_SCALE
        pr = jax.nn.softmax(jnp.where(mask, s, NEG), axis=-1)
        return jnp.einsum('bgrqk,bkgd->bqgrd', pr.astype(kvs.dtype), kvs[:, :, 1])

    o = lax.map(block, jnp.arange(nqb))
    o = jnp.moveaxis(o, 0, 1).reshape(B, Tp, NSA_G, NSA_R, NSA_HD)[:, :T]
    return o, kv_cat


def _sel_attention_paged(qg, cache_l, page_table, kv_new, imp, pos0):
    B, T = qg.shape[:2]
    page = cache_l.shape[1]
    past = page_table.shape[1] * page
    ns = imp.shape[-1]
    n_sel = min(N_SEL, ns)
    n_past_blk = past // SEL_BLOCK
    bpp = page // SEL_BLOCK
    assert page % SEL_BLOCK == 0 and T <= SEL_BLOCK and ns == n_past_blk + 1 and pos0 == past
    qpos = pos0 + jnp.arange(T)
    cur = qpos // SEL_BLOCK
    blk = jnp.arange(ns)
    sc = jnp.where(blk[None, :] <= cur[:, None], imp, -jnp.inf)
    sc = jnp.where(blk[None, :] == cur[:, None], jnp.inf, sc)
    _, idx = lax.top_k(sc, n_sel)
    idc = jnp.minimum(idx, n_past_blk - 1)
    pp = page_table[jnp.arange(B)[:, None, None, None], idc // bpp]
    hh = idc % bpp
    gi = jnp.arange(NSA_G)[None, :, None, None]
    def block_of_page(c):
        pg = cache_l[pp, :, c, gi, :]
        out = pg[..., :SEL_BLOCK, :]
        for i in range(1, bpp):
            out = jnp.where((hh == i)[..., None, None], pg[..., i * SEL_BLOCK:(i + 1) * SEL_BLOCK, :], out)
        return out

    kg = block_of_page(2)
    vg = block_of_page(3)
    padn = ((0, 0), (0, SEL_BLOCK - T), (0, 0), (0, 0))
    knew = jnp.pad(kv_new[:, :, 2], padn).transpose(0, 2, 1, 3)[:, :, None, None]
    vnew = jnp.pad(kv_new[:, :, 3], padn).transpose(0, 2, 1, 3)[:, :, None, None]
    is_new = (idx >= n_past_blk)[..., None, None]
    kg = jnp.where(is_new, knew, kg)
    vg = jnp.where(is_new, vnew, vg)
    s = jnp.einsum('bqgrd,bgqnld->bgrqnl', qg, kg).astype(jnp.float32) * NSA_SCALE
    kpos = idx[..., None] * SEL_BLOCK + jnp.arange(SEL_BLOCK)
    valid = kpos <= qpos[None, None, :, None, None]
    s = jnp.where(valid[:, :, None], s, NEG)
    pr = jax.nn.softmax(s.reshape(B, NSA_G, NSA_R, T, n_sel * SEL_BLOCK), axis=-1).reshape(s.shape)
    return jnp.einsum('bgrqnl,bgqnld->bqgrd', pr.astype(vg.dtype), vg)


def _nsa_paged(p, cache_l, page_table, win_buf, pos0, pe, w1, w2):
    B, T, _ = p.shape
    past = page_table.shape[1] * cache_l.shape[1]
    q, kv6, gl = _split(p, (BRANCH_W, NSA_KV_COLS, 3 * NSA_H))
    qg = q.reshape(B, T, NSA_G, NSA_R, NSA_HD)
    kv6 = kv6.reshape(B, T, 6, NSA_G, NSA_HD)
    kv_new = kv6[:, :, :4]
    nc = (past + T - CMP_BLOCK) // CMP_STRIDE + 1
    assert nc == (past - CMP_BLOCK) // CMP_STRIDE + 1
    ck = cache_l[page_table, :, 0:2].reshape(B, past, 2, NSA_G, NSA_HD)
    kc = _compress(ck[:, :, 0], pe[0], w1[0], w2[0])
    vc = _compress(ck[:, :, 1], pe[1], w1[1], w2[1])
    o_cmp, p_cmp = _cmp_attention(qg, kc, vc, pos0)
    ns = -(-(past + T) // SEL_BLOCK)
    imp = jnp.einsum('bgrtc,cs->bgts', p_cmp, _overlap(nc, ns))
    o_sel = _sel_attention_paged(qg, cache_l, page_table, kv_new, imp, pos0)
    o_win, win_cat = _win_attention(qg, kv6[:, :, 4:], win_buf, pos0)
    g = jax.nn.sigmoid(gl.reshape(B, T, 3, NSA_G, NSA_R))[..., None]
    o = g[:, :, 0] * o_cmp + g[:, :, 1] * o_sel + g[:, :, 2] * o_win
    return o.reshape(B, T, BRANCH_W), kv_new, win_cat


def _nsa_nopast(p, win_buf, pe, w1, w2, cdt=jnp.bfloat16, out_dtype=jnp.bfloat16):
    B, T, _ = p.shape
    q, kv6, gl = _split(p, (BRANCH_W, NSA_KV_COLS, 3 * NSA_H))
    kv6 = kv6.reshape(B, T, 6, NSA_G, NSA_HD)
    kv_new = kv6[:, :, :4]
    kc = _compress(kv6[:, :, 0], pe[0], w1[0], w2[0])
    vc = _compress(kv6[:, :, 1], pe[1], w1[1], w2[1])
    o = _nsa_prompt(q, kc, vc, kv6, gl, win_buf, cdt, out_dtype)
    win_cat = jnp.concatenate([win_buf, kv6[:, :, 4:]], axis=1)
    return o, kv_new, win_cat


def _prep_weights(w_in_t, ffn_down):
    return _gate_weights(w_in_t), ffn_down.astype(jnp.bfloat16)


def _layer(l, xpair, pos0, kv_past, win_buf, rw_S, rw_shift, pool_buf, hg_S, ffn_buf, wts, prm, lb_all, win_keep):
    (w_in_t, w_gate_b, w_branch, w_out, ffn_up, ffn_down_b) = wts
    x, x2b = xpair
    B, T, _ = x.shape
    m = B * T
    bf = jnp.bfloat16
    x2 = x.reshape(m, D_MODEL)

    def in_proj(col0, n):
        return _matmul_nt(x2b, w_in_t, l, 2048, IN_TN, col0, n)

    p_rw = in_proj(RW_OFF, RW_COLS).reshape(B, T, RW_COLS)
    p_pool = in_proj(POOL_OFF, BRANCH_W).reshape(B, T, BRANCH_W)
    p_hg = in_proj(HG_OFF, HG_COLS).reshape(B, T, HG_COLS)
    p_nsa = in_proj(NSA_OFF, NSA_COLS_PAD)[:, :NSA_COLS].reshape(B, T, NSA_COLS)
    h_rw, rw_S_new, rw_shift_new = _rwkv7(p_rw, rw_shift, rw_S, prm['rw_mu'][l], prm['rw_w0'][l], prm['rw_w2'][l],
                                          prm['rw_a0'][l], prm['rw_a2'][l], prm['rw_g2'][l], prm['rw_k_k'][l],
                                          prm['rw_k_a'][l], prm['rw_r_k'][l], prm['rw_ln_w'][l], prm['rw_ln_b'][l])
    h_pool, pool_new = _pool(p_pool, pool_buf, pos0, prm['pool_w'][l], prm['pool_scale'][l])
    h_hg, hg_S_new = _hgrn2_pallas(p_hg, hg_S.astype(jnp.float32), lb_all[l], prm['hg_norm_w'][l])
    if kv_past is None:
        h_nsa, kv_new, win_cat = _nsa_nopast(p_nsa, win_buf, prm['cmp_pe'][l], prm['cmp_w1'][l], prm['cmp_w2'][l])
    else:
        h_nsa, kv_new, win_cat = _nsa_paged(p_nsa, kv_past[0], kv_past[1], win_buf, pos0, prm['cmp_pe'][l],
                                            prm['cmp_w1'][l], prm['cmp_w2'][l])
    win_new = win_cat[:, -win_keep:]
    h = jnp.stack([t.reshape(m, BRANCH_W).astype(bf) for t in (h_rw, h_pool, h_hg, h_nsa)])
    merged = _gated_merge(x2b, w_gate_b, h, w_branch, l, 1024, 512)
    y = _matmul(merged.astype(bf), w_out, l, 1024, 512)
    x2, x2b = _residual_layer_norm(x2, y, prm['ln_w'][l, 0], prm['ln_b'][l, 0])
    up = _matmul(x2b, ffn_up, l, 1024, 512)
    hcv = _ffn_activation(up, ffn_buf, prm['ffn_conv'][l], T, 128)
    f = _matmul_ksplit(hcv, ffn_down_b, l, 1024, 512, D_FF // 2)
    x2, x2b = _residual_layer_norm(x2, f, prm['ln_w'][l, 1], prm['ln_b'][l, 1])
    assert T >= CONV_W - 1
    ffn_new = up.reshape(B, T, 2 * D_FF)[:, T - (CONV_W - 1):, :D_FF]
    return (x2.reshape(B, T, D_MODEL), x2b), (kv_new, win_new, rw_S_new, rw_shift_new, pool_new, hg_S_new, ffn_new)


def kernel(x_prompt, x_sample, cache_kv, page_table, state_win, state_rwkv, state_rwkv_shift, state_pool, state_hgrn, state_ffn_conv, w_in, rw_mu, rw_w0, rw_w2, rw_a0, rw_a2, rw_g2, rw_k_k, rw_k_a, rw_r_k, rw_ln_w, rw_ln_b, pool_w, pool_scale, hg_lb_raw, hg_norm_w, cmp_pe, cmp_w1, cmp_w2, w_branch, w_out, ln_w, ln_b, ffn_up, ffn_conv, ffn_down):
    prm = dict(rw_mu=rw_mu, rw_w0=rw_w0, rw_w2=rw_w2, rw_a0=rw_a0, rw_a2=rw_a2, rw_g2=rw_g2, rw_k_k=rw_k_k,
               rw_k_a=rw_k_a, rw_r_k=rw_r_k, rw_ln_w=rw_ln_w, rw_ln_b=rw_ln_b, pool_w=pool_w,
               pool_scale=pool_scale, hg_norm_w=hg_norm_w, cmp_pe=cmp_pe, cmp_w1=cmp_w1, cmp_w2=cmp_w2,
               ln_w=ln_w, ln_b=ln_b, ffn_conv=ffn_conv)
    lb_cum = jnp.cumsum(jax.nn.softmax(hg_lb_raw.astype(jnp.float32), axis=0), axis=0)
    lb_all = lb_cum - lb_cum[:1]
    past_len = page_table.shape[1] * cache_kv.shape[2]
    win_keep = state_win.shape[2]
    w_in_t = jnp.swapaxes(w_in, 1, 2)
    w_gate_b, ffn_down_b = _prep_weights(w_in_t, ffn_down)
    wts = (w_in_t, w_gate_b, w_branch, w_out, ffn_up, ffn_down_b)

    bp = x_prompt.shape[0]
    dt = x_prompt.dtype
    z_win = jnp.zeros((bp, WINDOW, 2, NSA_G, NSA_HD), dt)
    z_rw = jnp.zeros((bp, RW_H, RW_HD, RW_HD), dt)
    z_shift = jnp.zeros((bp, RW_COLS), dt)
    z_pool = jnp.zeros((bp, POOL_BUF, BRANCH_W), dt)
    z_hg = jnp.zeros((bp, HG_H, HG_DK, HG_DV), dt)
    z_ffn = jnp.zeros((bp, CONV_W - 1, D_FF), dt)
    xp = (x_prompt, x_prompt.reshape(-1, D_MODEL).astype(jnp.bfloat16))
    new_p = []
    for l in range(DEPTH):
        xp, st = _layer(l, xp, 0, None, z_win, z_rw, z_shift, z_pool, z_hg, z_ffn, wts, prm, lb_all, win_keep)
        new_p.append(st)

    db = x_sample.shape[0]
    xs = (x_sample, x_sample.reshape(-1, D_MODEL).astype(jnp.bfloat16))
    new_s = []
    for l in range(DEPTH):
        kv_past = (cache_kv[l], page_table)
        xs, st = _layer(l, xs, past_len, kv_past, state_win[l], state_rwkv[l], state_rwkv_shift[l],
                        state_pool[l], state_hgrn[l], state_ffn_conv[l], wts, prm, lb_all, win_keep)
        new_s.append(st)

    sp = [jnp.stack([st[i] for st in new_p]) for i in range(7)]
    ss = [jnp.stack([st[i] for st in new_s]) for i in range(7)]
    return (xp[0], xs[0], sp[0], ss[0], sp[1], ss[1], sp[2], ss[2], sp[3], ss[3], sp[4], ss[4], sp[5], ss[5], sp[6], ss[6])
```
